```python
import jax, jax.numpy as jnp
from jax import lax
import numpy as np

D_MODEL = 1024
BATCH = 2
SEQ = 8192
DEPTH = 1

HEAD_DIM = 64
D_MIX = D_MODEL
N_HEADS = D_MIX // HEAD_DIM
NSA_HEADS = N_HEADS // 2
NSA_KV = 2
NSA_GROUP = NSA_HEADS // NSA_KV
SWA_HEADS = N_HEADS - NSA_HEADS
SWA_KV = 1
SWA_GROUP = SWA_HEADS // SWA_KV
NSA_CMP_LEN = 32
NSA_CMP_STRIDE = 16
NSA_CMP_HIDDEN = 256
NSA_SEL_BLOCK = 64
NSA_TOPN = 16
NSA_WINDOW = 512
SWA_WINDOW = 128
Q_BLOCK = 128
FORCE_SCORE = 1.0e4
PEER_HEADS = 8
PEER_NKEYS = 128
PEER_EXPERTS = PEER_NKEYS * PEER_NKEYS
PEER_DKEY = 256
PEER_TOPK = 16
PEER_CHUNK = 128
EPS = 1e-6

kernel_name = 'hymba_nsa_swa_peer_layer'


def split_sizes():
    hd = HEAD_DIM
    return [NSA_HEADS * hd,
            NSA_KV * hd, NSA_KV * hd,
            NSA_KV * hd, NSA_KV * hd,
            NSA_KV * hd, NSA_KV * hd,
            NSA_HEADS * 3,
            SWA_HEADS * hd,
            SWA_KV * hd, SWA_KV * hd]


def alibi_slopes(n):
    return (2.0 ** (-8.0 * (np.arange(n) + 1) / n)).astype(np.float32)


def rms_norm(x, g):
    xf = x.astype(jnp.float32)
    y = xf * lax.rsqrt(jnp.mean(xf * xf, axis=-1, keepdims=True) + EPS)
    return (y * g.astype(jnp.float32)).astype(x.dtype)


def masked_softmax(s, valid):
    s = jnp.where(valid, s.astype(jnp.float32), -jnp.inf)
    m = jnp.max(s, axis=-1, keepdims=True)
    m = jnp.where(jnp.isfinite(m), m, 0.0)
    e = jnp.exp(s - m)
    return e / jnp.maximum(jnp.sum(e, axis=-1, keepdims=True), 1e-30)


def compress(kv, pe, w1, w2):
    B, T, G, dh = kv.shape
    n_cmp = (T - NSA_CMP_LEN) // NSA_CMP_STRIDE + 1
    idx = np.arange(n_cmp)[:, None] * NSA_CMP_STRIDE + np.arange(NSA_CMP_LEN)[None, :]
    blocks = kv[:, idx] + pe[None, None, :, None, :]
    flat = blocks.transpose(0, 1, 3, 2, 4).reshape(B, n_cmp, G, NSA_CMP_LEN * dh)
    return jax.nn.gelu(flat @ w1) @ w2


def nsa_attention(q, kc_raw, vc_raw, ks, vs, kw, vw, gates,
                  pe_k, w1_k, w2_k, pe_v, w1_v, w2_v, slopes):
    B, T, G, Hg, dh = q.shape
    scale = dh ** -0.5
    kc = compress(kc_raw, pe_k, w1_k, w2_k)
    vc = compress(vc_raw, pe_v, w1_v, w2_v)
    n_cmp = kc.shape[1]
    n_sel = T // NSA_SEL_BLOCK
    k_top = min(NSA_TOPN, n_sel)
    cmp_end = jnp.asarray(np.arange(n_cmp) * NSA_CMP_STRIDE + NSA_CMP_LEN - 1, jnp.int32)
    c0 = np.arange(n_cmp)[:, None] * NSA_CMP_STRIDE
    s0 = np.arange(n_sel)[None, :] * NSA_SEL_BLOCK
    overlap = jnp.asarray(((c0 < s0 + NSA_SEL_BLOCK) & (c0 + NSA_CMP_LEN > s0)).astype(np.float32))
    ks_blk = ks.reshape(B, n_sel, NSA_SEL_BLOCK, G, dh).transpose(0, 3, 1, 2, 4)
    vs_blk = vs.reshape(B, n_sel, NSA_SEL_BLOCK, G, dh).transpose(0, 3, 1, 2, 4)
    kw_pad = jnp.pad(kw, ((0, 0), (NSA_WINDOW, 0), (0, 0), (0, 0)))
    vw_pad = jnp.pad(vw, ((0, 0), (NSA_WINDOW, 0), (0, 0), (0, 0)))
    gsig = jax.nn.sigmoid(gates.astype(jnp.float32))
    sl = jnp.asarray(slopes).reshape(G, Hg)[None, :, :, None, None]
    bi = jnp.arange(B)[:, None, None, None]
    gi = jnp.arange(G)[None, :, None, None]

    def block(i):
        q0 = i * Q_BLOCK
        qb = lax.dynamic_slice_in_dim(q, q0, Q_BLOCK, axis=1)
        tq = q0 + jnp.arange(Q_BLOCK)
        dist_c = tq[:, None] - cmp_end[None, :]
        s_c = jnp.einsum('bqghd,bngd->bghqn', qb, kc).astype(jnp.float32) * scale
        p_c = masked_softmax(s_c - sl * dist_c.astype(jnp.float32), dist_c >= 0)
        o_c = jnp.einsum('bghqn,bngd->bqghd', p_c.astype(vc.dtype), vc)
        imp = jnp.einsum('bgqn,ns->bgqs', jnp.sum(p_c, axis=2), overlap)
        blk_t = (tq // NSA_SEL_BLOCK)[:, None]
        j = jnp.arange(n_sel)[None, :]
        imp = jnp.where((j == 0) | (j == blk_t) | (j == blk_t - 1), FORCE_SCORE, imp)
        imp = jnp.where(j <= blk_t, imp, -jnp.inf)
        top_s, top_i = lax.top_k(imp, k_top)
        blk_ok = jnp.isfinite(top_s)
        k_sel = ks_blk[bi, gi, top_i].reshape(B, G, Q_BLOCK, k_top * NSA_SEL_BLOCK, dh)
        v_sel = vs_blk[bi, gi, top_i].reshape(B, G, Q_BLOCK, k_top * NSA_SEL_BLOCK, dh)
        pos = (top_i[..., None] * NSA_SEL_BLOCK + jnp.arange(NSA_SEL_BLOCK)).reshape(
            B, G, Q_BLOCK, k_top * NSA_SEL_BLOCK)
        dist_s = tq[None, None, :, None] - pos
        ok_s = jnp.broadcast_to(blk_ok[..., None], blk_ok.shape + (NSA_SEL_BLOCK,)).reshape(
            dist_s.shape) & (dist_s >= 0)
        s_s = jnp.einsum('bqghd,bgqkd->bghqk', qb, k_sel).astype(jnp.float32) * scale
        s_s = s_s - sl * dist_s[:, :, None].astype(jnp.float32)
        p_s = masked_softmax(s_s, ok_s[:, :, None])
        o_s = jnp.einsum('bghqk,bgqkd->bqghd', p_s.astype(v_sel.dtype), v_sel)
        kwb = lax.dynamic_slice_in_dim(kw_pad, q0, Q_BLOCK + NSA_WINDOW, axis=1)
        vwb = lax.dynamic_slice_in_dim(vw_pad, q0, Q_BLOCK + NSA_WINDOW, axis=1)
        pos_w = q0 - NSA_WINDOW + jnp.arange(Q_BLOCK + NSA_WINDOW)
        dist_w = tq[:, None] - pos_w[None, :]
        ok_w = (dist_w >= 0) & (dist_w < NSA_WINDOW) & (pos_w >= 0)[None, :]
        s_w = jnp.einsum('bqghd,bkgd->bghqk', qb, kwb).astype(jnp.float32) * scale
        p_w = masked_softmax(s_w - sl * dist_w.astype(jnp.float32), ok_w)
        o_w = jnp.einsum('bghqk,bkgd->bqghd', p_w.astype(vwb.dtype), vwb)
        gb = lax.dynamic_slice_in_dim(gsig, q0, Q_BLOCK, axis=1)
        o = gb[..., 0:1] * o_c + gb[..., 1:2] * o_s + gb[..., 2:3] * o_w
        return o.astype(q.dtype)

    out = lax.map(block, jnp.arange(T // Q_BLOCK))
    return out.transpose(1, 0, 2, 3, 4, 5).reshape(B, T, G * Hg * dh)


def swa_attention(q, k, v, sinks, slopes):
    B, T, H, dh = q.shape
    KV = k.shape[2]
    Hg = H // KV
    nb = T // Q_BLOCK
    scale = dh ** -0.5
    qb = q.reshape(B, nb, Q_BLOCK, KV, Hg, dh)
    pad = ((0, 0), (Q_BLOCK, 0), (0, 0), (0, 0))
    kp = jnp.pad(k, pad).reshape(B, nb + 1, Q_BLOCK, KV, dh)
    vp = jnp.pad(v, pad).reshape(B, nb + 1, Q_BLOCK, KV, dh)
    kwin = jnp.concatenate([kp[:, :-1], kp[:, 1:]], axis=2)
    vwin = jnp.concatenate([vp[:, :-1], vp[:, 1:]], axis=2)
    dist = (Q_BLOCK + np.arange(Q_BLOCK)[:, None] - np.arange(2 * Q_BLOCK)[None, :])
    kpos = np.arange(nb)[:, None] * Q_BLOCK - Q_BLOCK + np.arange(2 * Q_BLOCK)[None, :]
    valid = ((dist >= 0) & (dist < SWA_WINDOW))[None] & (kpos >= 0)[:, None, :]
    s = jnp.einsum('bnqghd,bnkgd->bnghqk', qb, kwin).astype(jnp.float32) * scale
    s = s - jnp.asarray(slopes).reshape(KV, Hg)[:, :, None, None] * dist.astype(np.float32)
    s = jnp.where(jnp.asarray(valid)[None, :, None, None], s, -jnp.inf)
    sink = jnp.broadcast_to(sinks.astype(jnp.float32).reshape(1, 1, KV, Hg, 1, 1), s.shape[:-1] + (1,))
    p = jax.nn.softmax(jnp.concatenate([s, sink], axis=-1), axis=-1)[..., :-1]
    o = jnp.einsum('bnghqk,bnkgd->bnqghd', p.astype(vwin.dtype), vwin)
    return o.reshape(B, T, H * dh)


def peer_ffn(h, wq, keys, u, v):
    B, T, D = h.shape
    q = (h @ wq).reshape(B, T, PEER_HEADS, 2, PEER_DKEY // 2)
    s = jnp.einsum('bthcd,hcnd->bthcn', q, keys).astype(jnp.float32)
    s1, i1 = lax.top_k(s[..., 0, :], PEER_TOPK)
    s2, i2 = lax.top_k(s[..., 1, :], PEER_TOPK)
    cand = (s1[..., :, None] + s2[..., None, :]).reshape(B, T, PEER_HEADS, PEER_TOPK * PEER_TOPK)
    cidx = (i1[..., :, None] * PEER_NKEYS + i2[..., None, :]).reshape(cand.shape)
    cs, ci = lax.top_k(cand, PEER_TOPK)
    eidx = jnp.take_along_axis(cidx, ci, axis=-1)
    gate = jax.nn.softmax(cs, axis=-1)
    n_tok = B * T
    n_chunk = n_tok // PEER_CHUNK
    n_sel = PEER_HEADS * PEER_TOPK
    hf = h.reshape(n_chunk, PEER_CHUNK, D)
    ef = eidx.reshape(n_chunk, PEER_CHUNK, n_sel)
    gf = gate.reshape(n_chunk, PEER_CHUNK, n_sel)

    def chunk(args):
        hc, ec, gc = args
        uc = u[ec]
        act = jax.nn.gelu(jnp.einsum('cd,ced->ce', hc, uc).astype(jnp.float32))
        vc = v[ec]
        return jnp.einsum('ce,ced->cd', (gc * act).astype(vc.dtype), vc)

    out = lax.map(chunk, (hf, ef, gf))
    return out.reshape(B, T, D).astype(h.dtype)


def setup_inputs(seed: int = 0) -> dict:
    key = jax.random.key(seed)
    ks = jax.random.split(key, 20)
    d_in = sum(split_sizes())
    hd = HEAD_DIM
    L = NSA_CMP_LEN
    nrm = lambda k, shape, s: jax.random.normal(k, shape, jnp.float32) * s
    return {
        'x': jax.random.normal(ks[0], (BATCH, SEQ, D_MODEL), jnp.float32),
        'ln1_g': 1.0 + nrm(ks[1], (DEPTH, D_MODEL), 0.02),
        'w_in': nrm(ks[2], (DEPTH, D_MODEL, d_in), D_MODEL ** -0.5),
        'cmp_pe_k': nrm(ks[3], (DEPTH, L, hd), 0.1),
        'cmp_w1_k': nrm(ks[4], (DEPTH, L * hd, NSA_CMP_HIDDEN), (L * hd) ** -0.5),
        'cmp_w2_k': nrm(ks[5], (DEPTH, NSA_CMP_HIDDEN, hd), NSA_CMP_HIDDEN ** -0.5),
        'cmp_pe_v': nrm(ks[6], (DEPTH, L, hd), 0.1),
        'cmp_w1_v': nrm(ks[7], (DEPTH, L * hd, NSA_CMP_HIDDEN), (L * hd) ** -0.5),
        'cmp_w2_v': nrm(ks[8], (DEPTH, NSA_CMP_HIDDEN, hd), NSA_CMP_HIDDEN ** -0.5),
        'swa_sinks': nrm(ks[9], (DEPTH, SWA_HEADS), 0.5),
        'w_out': nrm(ks[10], (DEPTH, D_MIX, D_MODEL), D_MIX ** -0.5),
        'ln2_g': 1.0 + nrm(ks[11], (DEPTH, D_MODEL), 0.02),
        'peer_wq': nrm(ks[12], (DEPTH, D_MODEL, PEER_HEADS * PEER_DKEY), D_MODEL ** -0.5),
        'peer_keys': nrm(ks[13], (DEPTH, PEER_HEADS, 2, PEER_NKEYS, PEER_DKEY // 2), (PEER_DKEY // 2) ** -0.5),
        'peer_u': nrm(ks[14], (DEPTH, PEER_EXPERTS, D_MODEL), D_MODEL ** -0.5),
        'peer_v': nrm(ks[15], (DEPTH, PEER_EXPERTS, D_MODEL), PEER_HEADS ** -0.5),
        'lnf_g': 1.0 + nrm(ks[16], (D_MODEL,), 0.02),
    }


def reference(x, ln1_g, w_in, cmp_pe_k, cmp_w1_k, cmp_w2_k, cmp_pe_v, cmp_w1_v, cmp_w2_v,
              swa_sinks, w_out, ln2_g, peer_wq, peer_keys, peer_u, peer_v, lnf_g):
    B, T, _ = x.shape
    hd = HEAD_DIM
    split_idx = [int(c) for c in np.cumsum(split_sizes())[:-1]]
    slopes = alibi_slopes(N_HEADS)
    swa_slopes = slopes[:SWA_HEADS]
    nsa_slopes = slopes[SWA_HEADS:]
    h = x
    for l in range(DEPTH):
        a = rms_norm(h, ln1_g[l])
        proj = a @ w_in[l]
        q_n, kc, vc, kss, vss, kw, vw, gt, q_s, k_s, v_s = jnp.split(proj, split_idx, axis=-1)
        kv4 = lambda t: t.reshape(B, T, NSA_KV, hd)
        o_n = nsa_attention(q_n.reshape(B, T, NSA_KV, NSA_GROUP, hd),
                            kv4(kc), kv4(vc), kv4(kss), kv4(vss), kv4(kw), kv4(vw),
                            gt.reshape(B, T, NSA_KV, NSA_GROUP, 3),
                            cmp_pe_k[l], cmp_w1_k[l], cmp_w2_k[l],
                            cmp_pe_v[l], cmp_w1_v[l], cmp_w2_v[l], nsa_slopes)
        o_s = swa_attention(q_s.reshape(B, T, SWA_HEADS, hd),
                            k_s.reshape(B, T, SWA_KV, hd), v_s.reshape(B, T, SWA_KV, hd),
                            swa_sinks[l], swa_slopes)
        h = h + jnp.concatenate([o_n, o_s], axis=-1) @ w_out[l]
        h = h + peer_ffn(rms_norm(h, ln2_g[l]), peer_wq[l], peer_keys[l], peer_u[l], peer_v[l])
    return rms_norm(h, lnf_g)
```

```python
import functools

import numpy as np
import jax
import jax.numpy as jnp
from jax import lax
from jax.experimental import pallas as pl
from jax.experimental.pallas import tpu as pltpu

F32 = jnp.float32
BF16 = jnp.bfloat16

HEAD_DIM = 64
N_HEADS = 16
NSA_HEADS = 8
NSA_KV = 2
NSA_GROUP = 4
SWA_HEADS = 8
NSA_CMP_LEN = 32
NSA_CMP_STRIDE = 16
NSA_CMP_HIDDEN = 256
NSA_SEL_BLOCK = 64
NSA_TOPN = 16
NSA_WINDOW = 512
SWA_WINDOW = 128
FORCE_SCORE = 1.0e4
PEER_HEADS = 8
PEER_NKEYS = 128
PEER_TOPK = 16
EPS = 1e-6
NEG = -1.0e30
LANES = 128
VMEM_LIMIT = 56 * 1024 * 1024


def _dot(a, b):
    return jnp.dot(a, b, preferred_element_type=F32)


def _dot_nt(a, b):
    return lax.dot_general(a, b, (((1,), (1,)), ((), ())), preferred_element_type=F32)


def _cparams(sem):
    return pltpu.CompilerParams(dimension_semantics=sem, vmem_limit_bytes=VMEM_LIMIT)


def _topk_axis0(v, k, n):
    rid = lax.broadcasted_iota(jnp.int32, v.shape, 0).astype(F32)
    out = []
    for _ in range(k):
        m = jnp.max(v, axis=0, keepdims=True)
        idx = jnp.min(jnp.where(v == m, rid, float(n)), axis=0, keepdims=True)
        out.append((m, idx))
        v = jnp.where(rid == idx, NEG, v)
    return out


def _proj_kernel(x_ref, g_ref, w_ref, obf_ref, of_ref, *, n_bf, chunk):
    x = x_ref[...]
    ms = jnp.mean(x * x, axis=-1, keepdims=True)
    a = ((x * lax.rsqrt(ms + EPS)) * g_ref[...]).astype(BF16)
    for c0 in range(0, n_bf, chunk):
        obf_ref[:, c0:c0 + chunk] = _dot(a, w_ref[:, c0:c0 + chunk]).astype(BF16)
    of_ref[...] = _dot(a, w_ref[:, n_bf:])


def _proj(x2, g, w_all, n_bf, tm):
    bt, d = x2.shape
    n_all = w_all.shape[1]
    n_f = n_all - n_bf
    chunk = 896 if n_bf % 896 == 0 else n_bf
    return pl.pallas_call(
        functools.partial(_proj_kernel, n_bf=n_bf, chunk=chunk),
        grid=(bt // tm,),
        in_specs=[pl.BlockSpec((tm, d), lambda i: (i, 0)),
                  pl.BlockSpec((1, d), lambda i: (0, 0)),
                  pl.BlockSpec((d, n_all), lambda i: (0, 0))],
        out_specs=[pl.BlockSpec((tm, n_bf), lambda i: (i, 0)),
                   pl.BlockSpec((tm, n_f), lambda i: (i, 0))],
        out_shape=[jax.ShapeDtypeStruct((bt, n_bf), BF16),
                   jax.ShapeDtypeStruct((bt, n_f), F32)],
        compiler_params=_cparams(("parallel",)),
        name="proj",
    )(x2, g, w_all)


def _compress_kernel(c_ref, pe_ref, w1_ref, w2_ref, o_ref, *, ncp):
    half = NSA_CMP_STRIDE * HEAD_DIM
    acc = None
    for g in range(NSA_KV):
        c = c_ref[g]
        xa = (c + pe_ref[0:1, :]).astype(BF16)
        xb = (c + pe_ref[1:2, :]).astype(BF16)
        ya = _dot(xa, w1_ref[:half, :])
        yb = _dot(xb, w1_ref[half:, :])
        hid = ya + pltpu.roll(yb, ncp - 1, 0)
        act = jax.nn.gelu(hid).astype(BF16)
        t = _dot(act, w2_ref[g])
        acc = t if acc is None else acc + t
    o_ref[...] = acc.astype(BF16)


def _compress(c_all, pe_all, w1_all, w2e):
    _, b, _, ncp, half = c_all.shape
    return pl.pallas_call(
        functools.partial(_compress_kernel, ncp=ncp),
        grid=(2, b),
        in_specs=[pl.BlockSpec((None, None, NSA_KV, ncp, half), lambda s, i: (s, i, 0, 0, 0)),
                  pl.BlockSpec((None, 2, half), lambda s, i: (s, 0, 0)),
                  pl.BlockSpec((None, 2 * half, NSA_CMP_HIDDEN), lambda s, i: (s, 0, 0)),
                  pl.BlockSpec((None, NSA_KV, NSA_CMP_HIDDEN, LANES), lambda s, i: (s, 0, 0, 0))],
        out_specs=pl.BlockSpec((None, None, ncp, LANES), lambda s, i: (s, i, 0, 0)),
        out_shape=jax.ShapeDtypeStruct((2, b, ncp, LANES), BF16),
        compiler_params=_cparams(("parallel", "parallel")),
        name="compress",
    )(c_all, pe_all, w1_all, w2e)


def _nsa_kernel(slopes_ref, q_ref, kv_ref, kc_ref, vc_ref, gate_ref, ovt_ref, o_ref,
                m_scr, l_scr, acc_scr, *, tq, tk, seq, ncp):
    i = pl.program_id(1)
    g = pl.program_id(2)
    q0 = i * tq
    n_sel = seq // NSA_SEL_BLOCK
    n_cmp = ncp - 1
    lane = lax.broadcasted_iota(jnp.int32, (tq, LANES), 1)
    lo_half = lane < HEAD_DIM

    kc = kc_ref[...]
    vc = vc_ref[...]
    tq_c = (q0 + lax.broadcasted_iota(jnp.int32, (tq, ncp), 0)).astype(F32)
    n_c = lax.broadcasted_iota(jnp.int32, (tq, ncp), 1)
    dist_c = tq_c - (n_c * NSA_CMP_STRIDE + (NSA_CMP_LEN - 1)).astype(F32)
    valid_c = (dist_c >= 0.0) & (n_c < n_cmp)
    psum = jnp.zeros((tq, ncp), F32)
    o_cmp = []
    for hh in range(NSA_GROUP):
        slope = slopes_ref[SWA_HEADS + NSA_GROUP * g + hh]
        qh = q_ref[:, hh * LANES:(hh + 1) * LANES]
        s = _dot_nt(qh, kc) - slope * dist_c
        s = jnp.where(valid_c, s, NEG)
        m = jnp.max(s, axis=1, keepdims=True)
        e = jnp.where(valid_c, jnp.exp(s - m), 0.0)
        den = jnp.maximum(jnp.sum(e, axis=1, keepdims=True), 1e-30)
        p = e / den
        psum = psum + p
        o_cmp.append(_dot(p.astype(BF16), vc))

    imp = _dot_nt(ovt_ref[...], psum.astype(BF16))
    jb = lax.broadcasted_iota(jnp.int32, (n_sel, tq), 0)
    blk_t = (q0 + lax.broadcasted_iota(jnp.int32, (n_sel, tq), 1)) // NSA_SEL_BLOCK
    imp = jnp.where((jb == 0) | (jb == blk_t) | (jb == blk_t - 1), FORCE_SCORE, imp)
    imp = jnp.where(jb <= blk_t, imp, NEG)
    jbf = jb.astype(F32)
    sel_t = jnp.zeros((n_sel, tq), F32)
    for m_r, idx_r in _topk_axis0(imp, min(NSA_TOPN, n_sel), n_sel):
        sel_t = jnp.where((jbf == idx_r) & (m_r > 0.5 * NEG), 1.0, sel_t)
    sel = sel_t.T.astype(BF16)

    m_scr[...] = jnp.full(m_scr.shape, NEG, F32)
    l_scr[...] = jnp.zeros(l_scr.shape, F32)
    acc_scr[...] = jnp.zeros(acc_scr.shape, F32)
    rel = (lax.broadcasted_iota(jnp.int32, (tq, tk), 0)
           - lax.broadcasted_iota(jnp.int32, (tq, tk), 1)).astype(F32)
    e_rel = (lax.broadcasted_iota(jnp.int32, (n_sel, tk), 0)
             - lax.broadcasted_iota(jnp.int32, (n_sel, tk), 1) // NSA_SEL_BLOCK)

    def update(br, hh, s, valid, v):
        s = jnp.where(valid, s, NEG)
        m_old = m_scr[br, hh]
        m_new = jnp.maximum(m_old, jnp.max(s, axis=1, keepdims=True))
        alpha = jnp.exp(m_old - m_new)
        p = jnp.where(valid, jnp.exp(s - m_new[:, 0:1]), 0.0)
        l_scr[br, hh] = alpha * l_scr[br, hh] + jnp.sum(p, axis=1, keepdims=True)
        acc_scr[br, hh] = alpha * acc_scr[br, hh] + _dot(p.astype(BF16), v)
        m_scr[br, hh] = m_new

    def body(j, carry):
        k0 = pl.multiple_of(j * tk, tk)
        kvt = kv_ref[pl.ds(k0, tk), :]
        ks, vs = kvt[:, 0:LANES], kvt[:, LANES:2 * LANES]
        kw, vw = kvt[:, 2 * LANES:3 * LANES], kvt[:, 3 * LANES:4 * LANES]
        dist = rel + (q0 - k0).astype(F32)
        causal = dist >= 0.0
        expand = (e_rel == k0 // NSA_SEL_BLOCK).astype(BF16)
        valid_s = (_dot(sel, expand) > 0.5) & causal
        for hh in range(NSA_GROUP):
            slope = slopes_ref[SWA_HEADS + NSA_GROUP * g + hh]
            qh = q_ref[:, hh * LANES:(hh + 1) * LANES]
            update(0, hh, _dot_nt(qh, ks) - slope * dist, valid_s, vs)

        @pl.when(k0 + tk > q0 - (NSA_WINDOW - 1))
        def _():
            valid_w = causal & (dist < float(NSA_WINDOW))
            for hh in range(NSA_GROUP):
                slope = slopes_ref[SWA_HEADS + NSA_GROUP * g + hh]
                qh = q_ref[:, hh * LANES:(hh + 1) * LANES]
                update(1, hh, _dot_nt(qh, kw) - slope * dist, valid_w, vw)

        return carry

    lax.fori_loop(0, (q0 + tq + tk - 1) // tk, body, 0)

    gs = jax.nn.sigmoid(gate_ref[...])
    is_g0 = g == 0
    heads = []
    for hh in range(NSA_GROUP):
        o = (gs[:, 3 * hh:3 * hh + 1] * o_cmp[hh]
             + gs[:, 3 * hh + 1:3 * hh + 2] * (acc_scr[0, hh] / l_scr[0, hh])
             + gs[:, 3 * hh + 2:3 * hh + 3] * (acc_scr[1, hh] / l_scr[1, hh]))
        heads.append(o)
    for pr in range(NSA_GROUP // 2):
        a, b = heads[2 * pr], heads[2 * pr + 1]
        a_lo = jnp.where(is_g0, a, pltpu.roll(a, HEAD_DIM, 1))
        b_hi = jnp.where(is_g0, pltpu.roll(b, HEAD_DIM, 1), b)
        o_ref[:, pr * LANES:(pr + 1) * LANES] = jnp.where(lo_half, a_lo, b_hi).astype(BF16)


def _nsa(slopes, pbf, pf, kcmp, ovt, batch, seq, tq, tk):
    nq = seq // tq
    ncp = kcmp.shape[2]
    n_sel = seq // NSA_SEL_BLOCK
    gw = NSA_GROUP * LANES
    kv_blk = 2 * NSA_HEADS * LANES // gw
    return pl.pallas_call(
        functools.partial(_nsa_kernel, tq=tq, tk=tk, seq=seq, ncp=ncp),
        grid=(batch, nq, NSA_KV),
        in_specs=[pl.BlockSpec(memory_space=pltpu.SMEM),
                  pl.BlockSpec((tq, gw), lambda b, i, g: (b * nq + i, g)),
                  pl.BlockSpec((seq, gw), lambda b, i, g: (b, kv_blk)),
                  pl.BlockSpec((None, None, ncp, LANES), lambda b, i, g: (0, b, 0, 0)),
                  pl.BlockSpec((None, None, ncp, LANES), lambda b, i, g: (1, b, 0, 0)),
                  pl.BlockSpec((tq, LANES), lambda b, i, g: (b * nq + i, 2 + g)),
                  pl.BlockSpec((n_sel, ncp), lambda b, i, g: (0, 0))],
        out_specs=pl.BlockSpec((tq, NSA_GROUP * HEAD_DIM), lambda b, i, g: (b * nq + i, g)),
        out_shape=jax.ShapeDtypeStruct((batch * seq, NSA_HEADS * HEAD_DIM), BF16),
        scratch_shapes=[pltpu.VMEM((2, NSA_GROUP, tq, LANES), F32),
                        pltpu.VMEM((2, NSA_GROUP, tq, LANES), F32),
                        pltpu.VMEM((2, NSA_GROUP, tq, LANES), F32)],
        compiler_params=_cparams(("parallel", "parallel", "arbitrary")),
        name="nsa",
    )(slopes, pbf, pbf, kcmp, kcmp, pf, ovt)


def _swa_kernel(slopes_ref, sinks_ref, q_ref, kv_ref, o_ref, *, tq):
    i = pl.program_id(1)
    q0 = i * tq
    kwid = tq + SWA_WINDOW
    start = pl.multiple_of(jnp.maximum(q0 - SWA_WINDOW, 0), SWA_WINDOW)
    kvw = kv_ref[pl.ds(start, kwid), :]
    dist = ((q0 - start) + lax.broadcasted_iota(jnp.int32, (tq, kwid), 0)
            - lax.broadcasted_iota(jnp.int32, (tq, kwid), 1)).astype(F32)
    valid = (dist >= 0.0) & (dist < float(SWA_WINDOW))
    lo_half = lax.broadcasted_iota(jnp.int32, (tq, LANES), 1) < HEAD_DIM
    outs = []
    for h in range(SWA_HEADS):
        qh = q_ref[:, h * LANES:(h + 1) * LANES]
        s = _dot_nt(qh, kvw) - slopes_ref[h] * dist
        s = jnp.where(valid, s, NEG)
        sink = sinks_ref[h]
        m = jnp.maximum(jnp.max(s, axis=1, keepdims=True), sink)
        e = jnp.where(valid, jnp.exp(s - m), 0.0)
        den = jnp.sum(e, axis=1, keepdims=True) + jnp.exp(sink - m)
        outs.append(_dot(e.astype(BF16), kvw) / den)
    for pr in range(SWA_HEADS // 2):
        a_lo = pltpu.roll(outs[2 * pr], HEAD_DIM, 1)
        o_ref[:, pr * LANES:(pr + 1) * LANES] = jnp.where(lo_half, a_lo, outs[2 * pr + 1]).astype(BF16)


def _swa(slopes, sinks, pbf, batch, seq, tq):
    nq = seq // tq
    qw = SWA_HEADS * LANES
    kv_blk = (2 * NSA_HEADS * LANES + 4 * LANES) // LANES
    return pl.pallas_call(
        functools.partial(_swa_kernel, tq=tq),
        grid=(batch, nq),
        in_specs=[pl.BlockSpec(memory_space=pltpu.SMEM),
                  pl.BlockSpec(memory_space=pltpu.SMEM),
                  pl.BlockSpec((tq, qw), lambda b, i: (b * nq + i, 1)),
                  pl.BlockSpec((seq, LANES), lambda b, i: (b, kv_blk))],
        out_specs=pl.BlockSpec((tq, SWA_HEADS * HEAD_DIM), lambda b, i: (b * nq + i, 0)),
        out_shape=jax.ShapeDtypeStruct((batch * seq, SWA_HEADS * HEAD_DIM), BF16),
        compiler_params=_cparams(("parallel", "parallel")),
        name="swa",
    )(slopes, sinks, pbf, pbf)


def _mid_kernel(x_ref, on_ref, os_ref, wo_ref, g2_ref, wq_ref, keys_ref,
                h_ref, h2_ref, st_ref):
    half = on_ref.shape[1]
    h = x_ref[...] + _dot(on_ref[...], wo_ref[:half, :]) + _dot(os_ref[...], wo_ref[half:, :])
    h_ref[...] = h
    ms = jnp.mean(h * h, axis=-1, keepdims=True)
    h2 = ((h * lax.rsqrt(ms + EPS)) * g2_ref[...]).astype(BF16)
    h2_ref[...] = h2
    for c in range(2 * PEER_HEADS):
        qc = _dot(h2, wq_ref[:, c * LANES:(c + 1) * LANES]).astype(BF16)
        st_ref[c] = _dot_nt(keys_ref[c], qc)


def _mid(x2, o_n, o_s, wo, g2, wq, keys, tm):
    bt, d = x2.shape
    nk = 2 * PEER_HEADS
    return pl.pallas_call(
        _mid_kernel,
        grid=(bt // tm,),
        in_specs=[pl.BlockSpec((tm, d), lambda i: (i, 0)),
                  pl.BlockSpec((tm, o_n.shape[1]), lambda i: (i, 0)),
                  pl.BlockSpec((tm, o_s.shape[1]), lambda i: (i, 0)),
                  pl.BlockSpec(wo.shape, lambda i: (0, 0)),
                  pl.BlockSpec((1, d), lambda i: (0, 0)),
                  pl.BlockSpec(wq.shape, lambda i: (0, 0)),
                  pl.BlockSpec(keys.shape, lambda i: (0, 0, 0))],
        out_specs=[pl.BlockSpec((tm, d), lambda i: (i, 0)),
                   pl.BlockSpec((tm, d), lambda i: (i, 0)),
                   pl.BlockSpec((nk, PEER_NKEYS, tm), lambda i: (0, 0, i))],
        out_shape=[jax.ShapeDtypeStruct((bt, d), F32),
                   jax.ShapeDtypeStruct((bt, d), BF16),
                   jax.ShapeDtypeStruct((nk, PEER_NKEYS, bt), F32)],
        compiler_params=_cparams(("parallel",)),
        name="mid",
    )(x2, o_n, o_s, wo, g2, wq, keys)


def _peer_topk_kernel(st_ref, a_ref, b_ref, g_ref,
                      v_scr, i_scr, cand_scr, af_scr, bf_scr, sa_scr, sb_scr, sg_scr):
    k = PEER_TOPK
    for h in range(PEER_HEADS):
        for c in range(2):
            for r, (m, idx) in enumerate(_topk_axis0(st_ref[2 * h + c], k, PEER_NKEYS)):
                v_scr[c, r:r + 1, :] = m
                i_scr[c, r:r + 1, :] = idx
        v2, i2 = v_scr[1], i_scr[1]
        for r in range(k):
            cand_scr[r * k:(r + 1) * k, :] = v_scr[0, r:r + 1, :] + v2
            af_scr[r * k:(r + 1) * k, :] = jnp.broadcast_to(i_scr[0, r:r + 1, :], v2.shape)
            bf_scr[r * k:(r + 1) * k, :] = i2
        cand = cand_scr[...]
        a_full, b_full = af_scr[...], bf_scr[...]
        rid = lax.broadcasted_iota(jnp.int32, cand.shape, 0).astype(F32)
        top = []
        for r in range(k):
            m = jnp.max(cand, axis=0, keepdims=True)
            f = jnp.min(jnp.where(cand == m, rid, float(k * k)), axis=0, keepdims=True)
            hit = rid == f
            sa_scr[h * k + r:h * k + r + 1, :] = jnp.max(jnp.where(hit, a_full, -1.0), axis=0, keepdims=True)
            sb_scr[h * k + r:h * k + r + 1, :] = jnp.max(jnp.where(hit, b_full, -1.0), axis=0, keepdims=True)
            top.append(m)
            cand = jnp.where(hit, NEG, cand)
        es = [jnp.exp(t - top[0]) for t in top]
        den = es[0]
        for e in es[1:]:
            den = den + e
        for r in range(k):
            sg_scr[h * k + r:h * k + r + 1, :] = es[r] / den
    a_ref[...] = sa_scr[...].T
    b_ref[...] = sb_scr[...].T
    g_ref[...] = sg_scr[...].T


def _peer_topk(st, tt):
    nk, _, bt = st.shape
    k = PEER_TOPK
    ns = PEER_HEADS * k
    out = jax.ShapeDtypeStruct((bt, ns), F32)
    spec = pl.BlockSpec((tt, ns), lambda i: (i, 0))
    return pl.pallas_call(
        _peer_topk_kernel,
        grid=(bt // tt,),
        in_specs=[pl.BlockSpec((nk, PEER_NKEYS, tt), lambda i: (0, 0, i))],
        out_specs=[spec, spec, spec],
        out_shape=[out, out, out],
        scratch_shapes=[pltpu.VMEM((2, k, tt), F32), pltpu.VMEM((2, k, tt), F32),
                        pltpu.VMEM((k * k, tt), F32), pltpu.VMEM((k * k, tt), F32),
                        pltpu.VMEM((k * k, tt), F32),
                        pltpu.VMEM((ns, tt), F32), pltpu.VMEM((ns, tt), F32),
                        pltpu.VMEM((ns, tt), F32)],
        compiler_params=_cparams(("parallel",)),
        name="peer_topk",
    )(st)


def _peer_w_kernel(a_ref, b_ref, g_ref, w_ref, s_scr, *, tt, pitch):
    nk = PEER_NKEYS
    sub = lax.broadcasted_iota(jnp.int32, (nk, a_ref.shape[1]), 0).astype(F32)

    def per_token(t, carry):
        a_row = a_ref[pl.ds(t, 1), :]
        b_row = b_ref[pl.ds(t, 1), :]
        g_row = g_ref[pl.ds(t, 1), :]
        x = jnp.where(a_row == sub, g_row, 0.0).astype(BF16)
        y = jnp.where(b_row == sub, 1.0, 0.0).astype(BF16)
        s_scr[pl.ds(t, nk, stride=pitch), :] = _dot_nt(x, y)
        return carry

    lax.fori_loop(0, tt, per_token, 0)

    def per_block(i1, carry):
        r0 = pl.multiple_of(i1 * pitch, 8)
        w_ref[i1] = s_scr[pl.ds(r0, tt), :].astype(BF16)
        return carry

    lax.fori_loop(0, nk, per_block, 0)


def _peer_w(a, b, g, tt):
    bt, ns = a.shape
    nk = PEER_NKEYS
    pitch = tt + 8
    spec = pl.BlockSpec((tt, ns), lambda i: (i, 0))
    return pl.pallas_call(
        functools.partial(_peer_w_kernel, tt=tt, pitch=pitch),
        grid=(bt // tt,),
        in_specs=[spec, spec, spec],
        out_specs=pl.BlockSpec((nk, tt, nk), lambda i: (0, i, 0)),
        out_shape=jax.ShapeDtypeStruct((nk, bt, nk), BF16),
        scratch_shapes=[pltpu.VMEM((nk * pitch, nk), F32)],
        compiler_params=_cparams(("parallel",)),
        name="peer_w",
    )(a, b, g)


def _peer_ffn_kernel(h2_ref, u_ref, v_ref, w_ref, h_ref, gf_ref, o_ref, acc_scr):
    j = pl.program_id(1)

    @pl.when(j == 0)
    def _():
        acc_scr[...] = jnp.zeros(acc_scr.shape, F32)

    act = _dot_nt(h2_ref[...], u_ref[...])
    nb = w_ref.shape[0]
    z = [(jax.nn.gelu(act[:, c * LANES:(c + 1) * LANES]) * w_ref[c].astype(F32)).astype(BF16)
         for c in range(nb)]
    acc_scr[...] += _dot(jnp.concatenate(z, axis=1), v_ref[...])

    @pl.when(j == pl.num_programs(1) - 1)
    def _():
        h = h_ref[...] + acc_scr[...]
        ms = jnp.mean(h * h, axis=-1, keepdims=True)
        o_ref[...] = (h * lax.rsqrt(ms + EPS)) * gf_ref[...]


def _peer_ffn(h2, u, v, w, h, gf, tm, te):
    bt, d = h.shape
    ne = u.shape[0]
    nb = te // LANES
    return pl.pallas_call(
        _peer_ffn_kernel,
        grid=(bt // tm, ne // te),
        in_specs=[pl.BlockSpec((tm, d), lambda i, j: (i, 0)),
                  pl.BlockSpec((te, d), lambda i, j: (j, 0)),
                  pl.BlockSpec((te, d), lambda i, j: (j, 0)),
                  pl.BlockSpec((nb, tm, LANES), lambda i, j: (j, i, 0)),
                  pl.BlockSpec((tm, d), lambda i, j: (i, 0)),
                  pl.BlockSpec((1, d), lambda i, j: (0, 0))],
        out_specs=pl.BlockSpec((tm, d), lambda i, j: (i, 0)),
        out_shape=jax.ShapeDtypeStruct((bt, d), F32),
        scratch_shapes=[pltpu.VMEM((tm, d), F32)],
        compiler_params=_cparams(("parallel", "arbitrary")),
        name="peer_ffn",
    )(h2, u, v, w, h, gf)


def _split_offsets():
    hd = HEAD_DIM
    sizes = [NSA_HEADS * hd, NSA_KV * hd, NSA_KV * hd, NSA_KV * hd, NSA_KV * hd, NSA_KV * hd,
             NSA_KV * hd, NSA_HEADS * 3, SWA_HEADS * hd, hd, hd]
    return [0] + [int(c) for c in np.cumsum(sizes)]


def _pick(n, prefs):
    for p in prefs:
        if n % p == 0:
            return p
    return n


def kernel(x, ln1_g, w_in, cmp_pe_k, cmp_w1_k, cmp_w2_k, cmp_pe_v, cmp_w1_v, cmp_w2_v, swa_sinks, w_out, ln2_g, peer_wq, peer_keys, peer_u, peer_v, lnf_g):
    batch, seq, d = x.shape
    bt = batch * seq
    hd = HEAD_DIM
    assert ln1_g.shape[0] == 1, "single layer"
    assert seq % 256 == 0
    slopes = jnp.asarray((2.0 ** (-8.0 * (np.arange(N_HEADS) + 1) / N_HEADS)).astype(np.float32))

    off = _split_offsets()
    w = w_in[0]
    col = lambda k: w[:, off[k]:off[k + 1]]
    scale = hd ** -0.5
    z64 = jnp.zeros((d, hd), F32)
    qn, qs = col(0) * scale, col(8) * scale
    qn_exp = []
    for h in range(NSA_HEADS):
        qh = qn[:, h * hd:(h + 1) * hd]
        qn_exp += [qh, z64] if h // NSA_GROUP == 0 else [z64, qh]
    qs_exp = []
    for h in range(SWA_HEADS):
        qs_exp += [qs[:, h * hd:(h + 1) * hd], z64]
    gt = col(7)
    gpad = jnp.zeros((d, LANES - NSA_GROUP * 3), F32)
    gcols = []
    for g in range(NSA_KV):
        gcols += [gt[:, g * NSA_GROUP * 3:(g + 1) * NSA_GROUP * 3], gpad]
    w_all = jnp.concatenate(qn_exp + qs_exp + [col(3), col(4), col(5), col(6), col(9), col(10),
                                               col(1), col(2)] + gcols, axis=1).astype(BF16)
    n_bf = (NSA_HEADS + SWA_HEADS) * LANES + 4 * LANES + LANES

    x2 = x.reshape(bt, d)
    pbf, pf = _proj(x2, ln1_g[0][None, :], w_all, n_bf, _pick(bt, (512, 256, 128)))

    ncp = seq // NSA_CMP_STRIDE
    c_all = pf[:, :2 * LANES].reshape(batch, ncp, NSA_CMP_STRIDE, 2, NSA_KV, hd)
    c_all = c_all.transpose(3, 0, 4, 1, 2, 5).reshape(2, batch, NSA_KV, ncp, NSA_CMP_STRIDE * hd)
    pe_all = jnp.stack([cmp_pe_k[0].reshape(2, -1), cmp_pe_v[0].reshape(2, -1)])
    w1_all = jnp.stack([cmp_w1_k[0], cmp_w1_v[0]]).astype(BF16)
    zc = jnp.zeros((NSA_CMP_HIDDEN, hd), F32)
    w2e = jnp.stack([jnp.stack([jnp.concatenate([w2, zc], axis=1), jnp.concatenate([zc, w2], axis=1)])
                     for w2 in (cmp_w2_k[0], cmp_w2_v[0])]).astype(BF16)
    kcmp = _compress(c_all, pe_all, w1_all, w2e)

    n_sel = seq // NSA_SEL_BLOCK
    c0 = np.arange(ncp)[None, :] * NSA_CMP_STRIDE
    s0 = np.arange(n_sel)[:, None] * NSA_SEL_BLOCK
    ovt = ((c0 < s0 + NSA_SEL_BLOCK) & (c0 + NSA_CMP_LEN > s0) & (np.arange(ncp)[None, :] < ncp - 1))
    ovt = jnp.asarray(ovt.astype(np.float32), BF16)

    o_n = _nsa(slopes, pbf, pf, kcmp, ovt, batch, seq, 256, 256)
    o_s = _swa(slopes, swa_sinks[0], pbf, batch, seq, 128)

    keys = peer_keys[0].reshape(2 * PEER_HEADS, PEER_NKEYS, -1).astype(BF16)
    h, h2, st = _mid(x2, o_n, o_s, w_out[0].astype(BF16), ln2_g[0][None, :],
                     peer_wq[0].astype(BF16), keys, _pick(bt, (256, 128)))
    a, b, g = _peer_topk(st, 128)
    wd = _peer_w(a, b, g, 128)
    out = _peer_ffn(h2, peer_u[0].astype(BF16), peer_v[0].astype(BF16), wd, h, lnf_g[None, :],
                    _pick(bt, (1024, 512, 256, 128)), 512)
    return out.reshape(batch, seq, d)
```

```python
import functools

import numpy as np
import jax
import jax.numpy as jnp
from jax import lax
from jax.experimental import pallas as pl
from jax.experimental.pallas import tpu as pltpu

F32 = jnp.float32
BF16 = jnp.bfloat16

HEAD_DIM = 64
N_HEADS = 16
NSA_HEADS = 8
NSA_KV = 2
NSA_GROUP = 4
SWA_HEADS = 8
NSA_CMP_LEN = 32
NSA_CMP_STRIDE = 16
NSA_CMP_HIDDEN = 256
NSA_SEL_BLOCK = 64
NSA_TOPN = 16
NSA_WINDOW = 512
SWA_WINDOW = 128
FORCE_SCORE = 1.0e4
PEER_HEADS = 8
PEER_NKEYS = 128
PEER_TOPK = 16
EPS = 1e-6
NEG = -1.0e30
LANES = 128
SUBLANES = 8
VMEM_LIMIT = 56 * 1024 * 1024


def _dot(a, b):
    return jnp.dot(a, b, preferred_element_type=F32)


def _dot_nt(a, b):
    return lax.dot_general(a, b, (((1,), (1,)), ((), ())), preferred_element_type=F32)


def _cparams(sem):
    return pltpu.CompilerParams(dimension_semantics=sem, vmem_limit_bytes=VMEM_LIMIT)


def _topk_axis0(v, k, code):
    big = 3.0e38
    out = []
    for _ in range(k):
        m = jnp.max(v, axis=0, keepdims=True)
        c = jnp.min(jnp.where(v == m, code, big), axis=0, keepdims=True)
        out.append((m, c))
        v = jnp.where(code == c, NEG, v)
    return out


def _proj_kernel(x_ref, g_ref, w_ref, obf_ref, vt_ref, of_ref, *, n_bf, n_vt, tk):
    x = x_ref[...]
    ms = jnp.mean(x * x, axis=-1, keepdims=True)
    a = ((x * lax.rsqrt(ms + EPS)) * g_ref[...]).astype(BF16)
    for c0 in range(0, n_bf, 4 * LANES):
        c1 = min(c0 + 4 * LANES, n_bf)
        obf_ref[:, c0:c1] = _dot(a, w_ref[:, c0:c1]).astype(BF16)
    yvt = _dot(a, w_ref[:, n_bf:n_bf + n_vt]).T
    for s in range(vt_ref.shape[0]):
        vt_ref[s] = yvt[:, s * tk:(s + 1) * tk].astype(BF16)
    of_ref[...] = _dot(a, w_ref[:, n_bf + n_vt:])


def _proj(x2, g, w_all, n_bf, n_vt, tm, tk):
    bt, d = x2.shape
    n_all = w_all.shape[1]
    n_f = n_all - n_bf - n_vt
    return pl.pallas_call(
        functools.partial(_proj_kernel, n_bf=n_bf, n_vt=n_vt, tk=tk),
        grid=(bt // tm,),
        in_specs=[pl.BlockSpec((tm, d), lambda i: (i, 0)),
                  pl.BlockSpec((1, d), lambda i: (0, 0)),
                  pl.BlockSpec((d, n_all), lambda i: (0, 0))],
        out_specs=[pl.BlockSpec((tm, n_bf), lambda i: (i, 0)),
                   pl.BlockSpec((tm // tk, n_vt, tk), lambda i: (i, 0, 0)),
                   pl.BlockSpec((tm, n_f), lambda i: (i, 0))],
        out_shape=[jax.ShapeDtypeStruct((bt, n_bf), BF16),
                   jax.ShapeDtypeStruct((bt // tk, n_vt, tk), BF16),
                   jax.ShapeDtypeStruct((bt, n_f), F32)],
        compiler_params=_cparams(("parallel",)),
        name="proj",
    )(x2, g, w_all)


def _compress_kernel(c_ref, pe_ref, w1_ref, w2_ref, o_ref, ot_ref, *, ncp):
    half = NSA_CMP_STRIDE * HEAD_DIM
    acc = None
    for g in range(NSA_KV):
        c = c_ref[g]
        xa = (c + pe_ref[0:1, :]).astype(BF16)
        xb = (c + pe_ref[1:2, :]).astype(BF16)
        ya = _dot(xa, w1_ref[:half, :])
        yb = _dot(xb, w1_ref[half:, :])
        hid = ya + pltpu.roll(yb, ncp - 1, 0)
        act = jax.nn.gelu(hid).astype(BF16)
        t = _dot(act, w2_ref[g])
        acc = t if acc is None else acc + t
    o_ref[...] = acc.astype(BF16)
    ot_ref[...] = acc.T.astype(BF16)


def _compress(c_all, pe_all, w1_all, w2e):
    _, b, _, ncp, half = c_all.shape
    return pl.pallas_call(
        functools.partial(_compress_kernel, ncp=ncp),
        grid=(2, b),
        in_specs=[pl.BlockSpec((None, None, NSA_KV, ncp, half), lambda s, i: (s, i, 0, 0, 0)),
                  pl.BlockSpec((None, 2, half), lambda s, i: (s, 0, 0)),
                  pl.BlockSpec((None, 2 * half, NSA_CMP_HIDDEN), lambda s, i: (s, 0, 0)),
                  pl.BlockSpec((None, NSA_KV, NSA_CMP_HIDDEN, LANES), lambda s, i: (s, 0, 0, 0))],
        out_specs=[pl.BlockSpec((None, None, ncp, LANES), lambda s, i: (s, i, 0, 0)),
                   pl.BlockSpec((None, None, LANES, ncp), lambda s, i: (s, i, 0, 0))],
        out_shape=[jax.ShapeDtypeStruct((2, b, ncp, LANES), BF16),
                   jax.ShapeDtypeStruct((2, b, LANES, ncp), BF16)],
        compiler_params=_cparams(("parallel", "parallel")),
        name="compress",
    )(c_all, pe_all, w1_all, w2e)


def _nsa_kernel(slopes_ref, q_ref, k_ref, vt_ref, kc_ref, vct_ref, gate_ref, ovt_ref, grp_ref, o_ref,
                qt_scr, oc_scr, cnt_scr, m_scr, l_scr, acc_scr, *, tq, tk, seq, ncp):
    i = pl.program_id(1)
    g = pl.program_id(2)
    q0 = i * tq
    n_sel = seq // NSA_SEL_BLOCK
    n_cmp = ncp - 1
    slope = [slopes_ref[SWA_HEADS + NSA_GROUP * g + hh] for hh in range(NSA_GROUP)]

    kc = kc_ref[...]
    vct = vct_ref[...]
    n_c = lax.broadcasted_iota(jnp.int32, (ncp, tq), 0)
    t_c = q0 + lax.broadcasted_iota(jnp.int32, (ncp, tq), 1)
    dist_c = (t_c - (n_c * NSA_CMP_STRIDE + (NSA_CMP_LEN - 1))).astype(F32)
    valid_c = (dist_c >= 0.0) & (n_c < n_cmp)
    psum = jnp.zeros((ncp, tq), F32)
    for hh in range(NSA_GROUP):
        qt = q_ref[:, hh * LANES:(hh + 1) * LANES].astype(F32).T.astype(BF16)
        qt_scr[hh] = qt
        s = jnp.where(valid_c, _dot(kc, qt) - slope[hh] * dist_c, NEG)
        m = jnp.max(s, axis=0, keepdims=True)
        e = jnp.where(valid_c, jnp.exp(s - m), 0.0)
        den = jnp.maximum(jnp.sum(e, axis=0, keepdims=True), 1e-30)
        p = e * (1.0 / den)
        psum = psum + p
        oc_scr[hh] = _dot(vct, p.astype(BF16))

    imp = _dot(ovt_ref[...], psum.astype(BF16))
    jb = lax.broadcasted_iota(jnp.int32, (n_sel, tq), 0)
    blk_t = (q0 + lax.broadcasted_iota(jnp.int32, (n_sel, tq), 1)) // NSA_SEL_BLOCK
    imp = jnp.where((jb == 0) | (jb == blk_t) | (jb == blk_t - 1), FORCE_SCORE, imp)
    imp = jnp.where(jb <= blk_t, imp, NEG)
    jbf = jb.astype(F32)
    sel_t = jnp.zeros((n_sel, tq), F32)
    for m_r, idx_r in _topk_axis0(imp, min(NSA_TOPN, n_sel), jbf):
        sel_t = jnp.where((jbf == idx_r) & (m_r > 0.5 * NEG), 1.0, sel_t)
    sel = sel_t.astype(BF16)
    cnt_scr[...] = _dot(grp_ref[...], sel)

    m_scr[...] = jnp.full(m_scr.shape, NEG, F32)
    l_scr[...] = jnp.zeros(l_scr.shape, F32)
    acc_scr[...] = jnp.zeros(acc_scr.shape, F32)
    rel = (lax.broadcasted_iota(jnp.int32, (tk, tq), 1)
           - lax.broadcasted_iota(jnp.int32, (tk, tq), 0)).astype(F32)
    e_rel = (lax.broadcasted_iota(jnp.int32, (tk, n_sel), 1)
             - lax.broadcasted_iota(jnp.int32, (tk, n_sel), 0) // NSA_SEL_BLOCK)

    def update(br, hh, s, valid, vt):
        r = br * NSA_GROUP + hh
        s = jnp.where(valid, s, NEG)
        m_old = m_scr[r:r + 1, :]
        m_new = jnp.maximum(m_old, jnp.max(s, axis=0, keepdims=True))
        alpha = jnp.exp(m_old - m_new)
        p = jnp.where(valid, jnp.exp(s - m_new), 0.0)
        l_scr[r:r + 1, :] = alpha * l_scr[r:r + 1, :] + jnp.sum(p, axis=0, keepdims=True)
        acc_scr[br, hh] = alpha * acc_scr[br, hh] + _dot(vt, p.astype(BF16))
        m_scr[r:r + 1, :] = m_new

    def body(j, carry):
        k0 = pl.multiple_of(j * tk, tk)
        dist = rel + (q0 - k0).astype(F32)
        causal = dist >= 0.0

        @pl.when(jnp.max(cnt_scr[pl.ds(j, 1), :]) > 0.5)
        def _():
            ks = k_ref[pl.ds(k0, tk), 0:LANES]
            vst = vt_ref[j, 0:LANES, :]
            expand = (e_rel == k0 // NSA_SEL_BLOCK).astype(BF16)
            valid_s = (_dot(expand, sel) > 0.5) & causal
            for hh in range(NSA_GROUP):
                update(0, hh, _dot(ks, qt_scr[hh]) - slope[hh] * dist, valid_s, vst)

        @pl.when(k0 + tk > q0 - (NSA_WINDOW - 1))
        def _():
            kw = k_ref[pl.ds(k0, tk), LANES:2 * LANES]
            vwt = vt_ref[j, LANES:2 * LANES, :]
            valid_w = causal & (dist < float(NSA_WINDOW))
            for hh in range(NSA_GROUP):
                update(1, hh, _dot(kw, qt_scr[hh]) - slope[hh] * dist, valid_w, vwt)

        return carry

    lax.fori_loop(0, (q0 + tq + tk - 1) // tk, body, 0)

    gst = jax.nn.sigmoid(gate_ref[...]).T
    r0 = pl.multiple_of(g * HEAD_DIM, HEAD_DIM)
    heads = []
    for hh in range(NSA_GROUP):
        inv_s = 1.0 / l_scr[hh:hh + 1, :]
        inv_w = 1.0 / l_scr[NSA_GROUP + hh:NSA_GROUP + hh + 1, :]
        heads.append(gst[3 * hh:3 * hh + 1, :] * oc_scr[hh, pl.ds(r0, HEAD_DIM), :]
                     + (gst[3 * hh + 1:3 * hh + 2, :] * inv_s) * acc_scr[0, hh, pl.ds(r0, HEAD_DIM), :]
                     + (gst[3 * hh + 2:3 * hh + 3, :] * inv_w) * acc_scr[1, hh, pl.ds(r0, HEAD_DIM), :])
    o_ref[...] = jnp.concatenate(heads, axis=0).T.astype(BF16)


def _nsa(slopes, pbf, vt3, pf, kcmp, kcmpt, ovt, grp, batch, seq, tq, tk, k_col):
    nq = seq // tq
    ncp = kcmp.shape[2]
    n_sel = seq // NSA_SEL_BLOCK
    nkt = seq // tk
    gw = NSA_GROUP * LANES
    return pl.pallas_call(
        functools.partial(_nsa_kernel, tq=tq, tk=tk, seq=seq, ncp=ncp),
        grid=(batch, nq, NSA_KV),
        in_specs=[pl.BlockSpec(memory_space=pltpu.SMEM),
                  pl.BlockSpec((tq, gw), lambda b, i, g: (b * nq + i, g)),
                  pl.BlockSpec((seq, 2 * LANES), lambda b, i, g: (b, k_col // (2 * LANES))),
                  pl.BlockSpec((nkt, 2 * LANES, tk), lambda b, i, g: (b, 0, 0)),
                  pl.BlockSpec((None, None, ncp, LANES), lambda b, i, g: (0, b, 0, 0)),
                  pl.BlockSpec((None, None, LANES, ncp), lambda b, i, g: (1, b, 0, 0)),
                  pl.BlockSpec((tq, LANES), lambda b, i, g: (b * nq + i, 2 + g)),
                  pl.BlockSpec((n_sel, ncp), lambda b, i, g: (0, 0)),
                  pl.BlockSpec((nkt, n_sel), lambda b, i, g: (0, 0))],
        out_specs=pl.BlockSpec((tq, NSA_GROUP * HEAD_DIM), lambda b, i, g: (b * nq + i, g)),
        out_shape=jax.ShapeDtypeStruct((batch * seq, NSA_HEADS * HEAD_DIM), BF16),
        scratch_shapes=[pltpu.VMEM((NSA_GROUP, LANES, tq), BF16),
                        pltpu.VMEM((NSA_GROUP, LANES, tq), F32),
                        pltpu.VMEM((nkt, tq), F32),
                        pltpu.VMEM((2 * NSA_GROUP, tq), F32),
                        pltpu.VMEM((2 * NSA_GROUP, tq), F32),
                        pltpu.VMEM((2, NSA_GROUP, LANES, tq), F32)],
        compiler_params=_cparams(("parallel", "parallel", "arbitrary")),
        name="nsa",
    )(slopes, pbf, pbf, vt3, kcmp, kcmpt, pf, ovt, grp)


def _swa_kernel(slopes_ref, sinks_ref, q_ref, kv_ref, o_ref, *, tq):
    i = pl.program_id(1)
    q0 = i * tq
    kwid = tq + SWA_WINDOW
    start = pl.multiple_of(jnp.maximum(q0 - SWA_WINDOW, 0), SWA_WINDOW)
    kvw = kv_ref[pl.ds(start, kwid), :]
    dist = ((q0 - start) + lax.broadcasted_iota(jnp.int32, (tq, kwid), 0)
            - lax.broadcasted_iota(jnp.int32, (tq, kwid), 1)).astype(F32)
    valid = (dist >= 0.0) & (dist < float(SWA_WINDOW))
    lo_half = lax.broadcasted_iota(jnp.int32, (tq, LANES), 1) < HEAD_DIM
    outs = []
    for h in range(SWA_HEADS):
        qh = q_ref[:, h * LANES:(h + 1) * LANES]
        s = _dot_nt(qh, kvw) - slopes_ref[h] * dist
        s = jnp.where(valid, s, NEG)
        sink = sinks_ref[h]
        m = jnp.maximum(jnp.max(s, axis=1, keepdims=True), sink)
        e = jnp.where(valid, jnp.exp(s - m), 0.0)
        den = jnp.sum(e, axis=1, keepdims=True) + jnp.exp(sink - m)
        outs.append(_dot(e.astype(BF16), kvw) / den)
    for pr in range(SWA_HEADS // 2):
        a_lo = pltpu.roll(outs[2 * pr], HEAD_DIM, 1)
        o_ref[:, pr * LANES:(pr + 1) * LANES] = jnp.where(lo_half, a_lo, outs[2 * pr + 1]).astype(BF16)


def _swa(slopes, sinks, pbf, batch, seq, tq, q_col, kv_col):
    nq = seq // tq
    qw = SWA_HEADS * LANES
    return pl.pallas_call(
        functools.partial(_swa_kernel, tq=tq),
        grid=(batch, nq),
        in_specs=[pl.BlockSpec(memory_space=pltpu.SMEM),
                  pl.BlockSpec(memory_space=pltpu.SMEM),
                  pl.BlockSpec((tq, qw), lambda b, i: (b * nq + i, q_col // qw)),
                  pl.BlockSpec((seq, LANES), lambda b, i: (b, kv_col // LANES))],
        out_specs=pl.BlockSpec((tq, SWA_HEADS * HEAD_DIM), lambda b, i: (b * nq + i, 0)),
        out_shape=jax.ShapeDtypeStruct((batch * seq, SWA_HEADS * HEAD_DIM), BF16),
        compiler_params=_cparams(("parallel", "parallel")),
        name="swa",
    )(slopes, sinks, pbf, pbf)


def _mid_kernel(x_ref, on_ref, os_ref, wo_ref, g2_ref, wq_ref, keys_ref,
                h_ref, h2_ref, st_ref):
    half = on_ref.shape[1]
    h = x_ref[...] + _dot(on_ref[...], wo_ref[:half, :]) + _dot(os_ref[...], wo_ref[half:, :])
    h_ref[...] = h
    ms = jnp.mean(h * h, axis=-1, keepdims=True)
    h2 = ((h * lax.rsqrt(ms + EPS)) * g2_ref[...]).astype(BF16)
    h2_ref[...] = h2
    for hp in range(PEER_HEADS):
        qh = _dot(h2, wq_ref[:, 2 * hp * LANES:2 * (hp + 1) * LANES]).astype(BF16)
        for c in range(2):
            st_ref[2 * hp + c] = _dot_nt(keys_ref[2 * hp + c], qh[:, c * LANES:(c + 1) * LANES])


def _mid(x2, o_n, o_s, wo, g2, wq, keys, tm):
    bt, d = x2.shape
    nk = 2 * PEER_HEADS
    return pl.pallas_call(
        _mid_kernel,
        grid=(bt // tm,),
        in_specs=[pl.BlockSpec((tm, d), lambda i: (i, 0)),
                  pl.BlockSpec((tm, o_n.shape[1]), lambda i: (i, 0)),
                  pl.BlockSpec((tm, o_s.shape[1]), lambda i: (i, 0)),
                  pl.BlockSpec(wo.shape, lambda i: (0, 0)),
                  pl.BlockSpec((1, d), lambda i: (0, 0)),
                  pl.BlockSpec(wq.shape, lambda i: (0, 0)),
                  pl.BlockSpec(keys.shape, lambda i: (0, 0, 0))],
        out_specs=[pl.BlockSpec((tm, d), lambda i: (i, 0)),
                   pl.BlockSpec((tm, d), lambda i: (i, 0)),
                   pl.BlockSpec((nk, PEER_NKEYS, tm), lambda i: (0, 0, i))],
        out_shape=[jax.ShapeDtypeStruct((bt, d), F32),
                   jax.ShapeDtypeStruct((bt, d), BF16),
                   jax.ShapeDtypeStruct((nk, PEER_NKEYS, bt), F32)],
        compiler_params=_cparams(("parallel",)),
        name="mid",
    )(x2, o_n, o_s, wo, g2, wq, keys)


def _pair_groups(k):
    groups = []
    i = 0
    while k // (i + 1) > 1:
        n = k // (i + 1)
        groups.append((i, n, -(-n // SUBLANES) * SUBLANES))
        i += 1
    return groups, i


def _peer_topk_kernel(st_ref, a_ref, b_ref, g_ref,
                      v_scr, i_scr, cand_scr, c_scr, f_scr, sa_scr, sb_scr, sg_scr):
    k = PEER_TOPK
    tt = st_ref.shape[2]
    groups, tail = _pair_groups(k)
    rid128 = lax.broadcasted_iota(jnp.int32, (PEER_NKEYS, tt), 0).astype(F32)
    codes, pads = [], []
    for (i, n, rows) in groups:
        j = lax.broadcasted_iota(jnp.int32, (rows, tt), 0)
        codes.append((j + i * k).astype(F32))
        pads.append(j < n)
    jt = lax.broadcasted_iota(jnp.int32, (k - tail, tt), 0)
    codes.append(((jt + tail) * k).astype(F32))
    code = jnp.concatenate(codes, axis=0)

    for h in range(PEER_HEADS):
        for c in range(2):
            for r, (m, idx) in enumerate(_topk_axis0(st_ref[2 * h + c], k, rid128)):
                v_scr[c, r:r + 1, :] = m
                i_scr[c, r:r + 1, :] = idx
        row = 0
        for gi, (i, n, rows) in enumerate(groups):
            vals = v_scr[0, i:i + 1, :] + v_scr[1, 0:rows, :]
            cand_scr[row:row + rows, :] = jnp.where(pads[gi], vals, NEG)
            row += rows
        cand_scr[row:row + k - tail, :] = v_scr[0, tail:k, :] + v_scr[1, 0:1, :]
        for r, (m, f) in enumerate(_topk_axis0(cand_scr[...], k, code)):
            c_scr[r:r + 1, :] = m
            f_scr[r:r + 1, :] = f
        cs, fl = c_scr[...], f_scr[...]
        fi = jnp.floor(fl * (1.0 / k))
        fj = fl - fi * k
        a = jnp.zeros_like(fl)
        b = jnp.zeros_like(fl)
        for r in range(k):
            a = jnp.where(fi == float(r), i_scr[0, r:r + 1, :], a)
            b = jnp.where(fj == float(r), i_scr[1, r:r + 1, :], b)
        e = jnp.exp(cs - cs[0:1, :])
        sa_scr[h * k:(h + 1) * k, :] = a
        sb_scr[h * k:(h + 1) * k, :] = b
        sg_scr[h * k:(h + 1) * k, :] = e / jnp.sum(e, axis=0, keepdims=True)
    a_ref[...] = sa_scr[...].T
    b_ref[...] = sb_scr[...].T
    g_ref[...] = sg_scr[...].T


def _peer_topk(st, tt):
    nk, _, bt = st.shape
    k = PEER_TOPK
    ns = PEER_HEADS * k
    groups, tail = _pair_groups(k)
    n_cand = sum(rows for _, _, rows in groups) + k - tail
    out = jax.ShapeDtypeStruct((bt, ns), F32)
    spec = pl.BlockSpec((tt, ns), lambda i: (i, 0))
    return pl.pallas_call(
        _peer_topk_kernel,
        grid=(bt // tt,),
        in_specs=[pl.BlockSpec((nk, PEER_NKEYS, tt), lambda i: (0, 0, i))],
        out_specs=[spec, spec, spec],
        out_shape=[out, out, out],
        scratch_shapes=[pltpu.VMEM((2, k, tt), F32), pltpu.VMEM((2, k, tt), F32),
                        pltpu.VMEM((n_cand, tt), F32),
                        pltpu.VMEM((k, tt), F32), pltpu.VMEM((k, tt), F32),
                        pltpu.VMEM((ns, tt), F32), pltpu.VMEM((ns, tt), F32),
                        pltpu.VMEM((ns, tt), F32)],
        compiler_params=_cparams(("parallel",)),
        name="peer_topk",
    )(st)


def _peer_w_kernel(a_ref, b_ref, g_ref, w_ref, s_scr, *, tt, pitch, unroll):
    nk = PEER_NKEYS
    sub = lax.broadcasted_iota(jnp.int32, (nk, a_ref.shape[1]), 0).astype(F32)

    def per_token(t, carry):
        a_row = a_ref[pl.ds(t, 1), :]
        b_row = b_ref[pl.ds(t, 1), :]
        g_row = g_ref[pl.ds(t, 1), :]
        x = jnp.where(a_row == sub, g_row, 0.0).astype(BF16)
        y = jnp.where(b_row == sub, 1.0, 0.0).astype(BF16)
        s_scr[pl.ds(t, nk, stride=pitch), :] = _dot_nt(x, y)
        return carry

    lax.fori_loop(0, tt, per_token, 0, unroll=unroll)

    def per_block(i1, carry):
        r0 = pl.multiple_of(i1 * pitch, SUBLANES)
        w_ref[i1] = s_scr[pl.ds(r0, tt), :].astype(BF16)
        return carry

    lax.fori_loop(0, nk, per_block, 0, unroll=unroll)


def _peer_w(a, b, g, tt):
    bt, ns = a.shape
    nk = PEER_NKEYS
    pitch = tt + SUBLANES
    spec = pl.BlockSpec((tt, ns), lambda i: (i, 0))
    return pl.pallas_call(
        functools.partial(_peer_w_kernel, tt=tt, pitch=pitch, unroll=8),
        grid=(bt // tt,),
        in_specs=[spec, spec, spec],
        out_specs=pl.BlockSpec((nk, tt, nk), lambda i: (0, i, 0)),
        out_shape=jax.ShapeDtypeStruct((nk, bt, nk), BF16),
        scratch_shapes=[pltpu.VMEM((nk * pitch, nk), F32)],
        compiler_params=_cparams(("parallel",)),
        name="peer_w",
    )(a, b, g)


def _peer_ffn_kernel(h2_ref, u_ref, v_ref, w_ref, h_ref, gf_ref, o_ref, acc_scr):
    j = pl.program_id(1)

    @pl.when(j == 0)
    def _():
        acc_scr[...] = jnp.zeros(acc_scr.shape, F32)

    act = _dot_nt(h2_ref[...], u_ref[...])
    nb = w_ref.shape[0]
    z = [(jax.nn.gelu(act[:, c * LANES:(c + 1) * LANES]) * w_ref[c].astype(F32)).astype(BF16)
         for c in range(nb)]
    acc_scr[...] += _dot(jnp.concatenate(z, axis=1), v_ref[...])

    @pl.when(j == pl.num_programs(1) - 1)
    def _():
        h = h_ref[...] + acc_scr[...]
        ms = jnp.mean(h * h, axis=-1, keepdims=True)
        o_ref[...] = (h * lax.rsqrt(ms + EPS)) * gf_ref[...]


def _peer_ffn(h2, u, v, w, h, gf, tm, te):
    bt, d = h.shape
    ne = u.shape[0]
    nb = te // LANES
    return pl.pallas_call(
        _peer_ffn_kernel,
        grid=(bt // tm, ne // te),
        in_specs=[pl.BlockSpec((tm, d), lambda i, j: (i, 0)),
                  pl.BlockSpec((te, d), lambda i, j: (j, 0)),
                  pl.BlockSpec((te, d), lambda i, j: (j, 0)),
                  pl.BlockSpec((nb, tm, LANES), lambda i, j: (j, i, 0)),
                  pl.BlockSpec((tm, d), lambda i, j: (i, 0)),
                  pl.BlockSpec((1, d), lambda i, j: (0, 0))],
        out_specs=pl.BlockSpec((tm, d), lambda i, j: (i, 0)),
        out_shape=jax.ShapeDtypeStruct((bt, d), F32),
        scratch_shapes=[pltpu.VMEM((tm, d), F32)],
        compiler_params=_cparams(("parallel", "arbitrary")),
        name="peer_ffn",
    )(h2, u, v, w, h, gf)


def _split_offsets():
    hd = HEAD_DIM
    sizes = [NSA_HEADS * hd, NSA_KV * hd, NSA_KV * hd, NSA_KV * hd, NSA_KV * hd, NSA_KV * hd,
             NSA_KV * hd, NSA_HEADS * 3, SWA_HEADS * hd, hd, hd]
    return [0] + [int(c) for c in np.cumsum(sizes)]


def _pick(n, prefs):
    for p in prefs:
        if n % p == 0:
            return p
    return n


def kernel(x, ln1_g, w_in, cmp_pe_k, cmp_w1_k, cmp_w2_k, cmp_pe_v, cmp_w1_v, cmp_w2_v, swa_sinks, w_out, ln2_g, peer_wq, peer_keys, peer_u, peer_v, lnf_g):
    batch, seq, d = x.shape
    bt = batch * seq
    hd = HEAD_DIM
    nsa_tq, nsa_tk = 256, 256
    assert ln1_g.shape[0] == 1, "single layer"
    assert seq % 512 == 0
    slopes = jnp.asarray((2.0 ** (-8.0 * (np.arange(N_HEADS) + 1) / N_HEADS)).astype(np.float32))

    off = _split_offsets()
    w = w_in[0]
    col = lambda k: w[:, off[k]:off[k + 1]]
    scale = hd ** -0.5
    z64 = jnp.zeros((d, hd), F32)
    qn, qs = col(0) * scale, col(8) * scale
    qn_exp = []
    for h in range(NSA_HEADS):
        qh = qn[:, h * hd:(h + 1) * hd]
        qn_exp += [qh, z64] if h // NSA_GROUP == 0 else [z64, qh]
    qs_exp = []
    for h in range(SWA_HEADS):
        qs_exp += [qs[:, h * hd:(h + 1) * hd], z64]
    gt = col(7)
    gpad = jnp.zeros((d, LANES - NSA_GROUP * 3), F32)
    gcols = []
    for g in range(NSA_KV):
        gcols += [gt[:, g * NSA_GROUP * 3:(g + 1) * NSA_GROUP * 3], gpad]
    w_all = jnp.concatenate(qn_exp + qs_exp + [col(3), col(5), col(9), col(10)]
                            + [col(4), col(6)] + [col(1), col(2)] + gcols, axis=1).astype(BF16)
    swa_q_col = NSA_HEADS * LANES
    nsa_k_col = swa_q_col + SWA_HEADS * LANES
    swa_kv_col = nsa_k_col + 2 * LANES
    n_bf = swa_kv_col + LANES
    n_vt = 2 * LANES

    x2 = x.reshape(bt, d)
    pbf, vt3, pf = _proj(x2, ln1_g[0][None, :], w_all, n_bf, n_vt, 512, nsa_tk)

    ncp = seq // NSA_CMP_STRIDE
    c_all = pf[:, :2 * LANES].reshape(batch, ncp, NSA_CMP_STRIDE, 2, NSA_KV, hd)
    c_all = c_all.transpose(3, 0, 4, 1, 2, 5).reshape(2, batch, NSA_KV, ncp, NSA_CMP_STRIDE * hd)
    pe_all = jnp.stack([cmp_pe_k[0].reshape(2, -1), cmp_pe_v[0].reshape(2, -1)])
    w1_all = jnp.stack([cmp_w1_k[0], cmp_w1_v[0]]).astype(BF16)
    zc = jnp.zeros((NSA_CMP_HIDDEN, hd), F32)
    w2e = jnp.stack([jnp.stack([jnp.concatenate([w2, zc], axis=1), jnp.concatenate([zc, w2], axis=1)])
                     for w2 in (cmp_w2_k[0], cmp_w2_v[0])]).astype(BF16)
    kcmp, kcmpt = _compress(c_all, pe_all, w1_all, w2e)

    n_sel = seq // NSA_SEL_BLOCK
    c0 = np.arange(ncp)[None, :] * NSA_CMP_STRIDE
    s0 = np.arange(n_sel)[:, None] * NSA_SEL_BLOCK
    ovt = ((c0 < s0 + NSA_SEL_BLOCK) & (c0 + NSA_CMP_LEN > s0) & (np.arange(ncp)[None, :] < ncp - 1))
    ovt = jnp.asarray(ovt.astype(np.float32), BF16)
    grp = (np.arange(n_sel)[None, :] * NSA_SEL_BLOCK // nsa_tk) == np.arange(seq // nsa_tk)[:, None]
    grp = jnp.asarray(grp.astype(np.float32), BF16)

    o_n = _nsa(slopes, pbf, vt3, pf, kcmp, kcmpt, ovt, grp, batch, seq, nsa_tq, nsa_tk, nsa_k_col)
    o_s = _swa(slopes, swa_sinks[0], pbf, batch, seq, 128, swa_q_col, swa_kv_col)

    keys = peer_keys[0].reshape(2 * PEER_HEADS, PEER_NKEYS, -1).astype(BF16)
    h, h2, st = _mid(x2, o_n, o_s, w_out[0].astype(BF16), ln2_g[0][None, :],
                     peer_wq[0].astype(BF16), keys, _pick(bt, (256, 128)))
    a, b, g = _peer_topk(st, 128)
    wd = _peer_w(a, b, g, 128)
    out = _peer_ffn(h2, peer_u[0].astype(BF16), peer_v[0].astype(BF16), wd, h, lnf_g[None, :],
                    _pick(bt, (1024, 512, 256, 128)), 512)
    return out.reshape(batch, seq, d)
```

```python
import functools

import numpy as np
import jax
import jax.numpy as jnp
from jax import lax
from jax.experimental import pallas as pl
from jax.experimental.pallas import tpu as pltpu

F32 = jnp.float32
BF16 = jnp.bfloat16

HEAD_DIM = 64
N_HEADS = 16
NSA_HEADS = 8
NSA_KV = 2
NSA_GROUP = 4
SWA_HEADS = 8
NSA_CMP_LEN = 32
NSA_CMP_STRIDE = 16
NSA_CMP_HIDDEN = 256
NSA_SEL_BLOCK = 64
NSA_TOPN = 16
NSA_WINDOW = 512
SWA_WINDOW = 128
FORCE_SCORE = 1.0e4
PEER_HEADS = 8
PEER_NKEYS = 128
PEER_TOPK = 16
EPS = 1e-6
NEG = -1.0e30
LANES = 128
SUBLANES = 8
BF16_ROWS = 16
VT_ROWS = HEAD_DIM + BF16_ROWS
LOG2E = 1.4426950408889634
VMEM_LIMIT = 56 * 1024 * 1024


def _dot(a, b):
    return jnp.dot(a, b, preferred_element_type=F32)


def _dot_nt(a, b):
    return lax.dot_general(a, b, (((1,), (1,)), ((), ())), preferred_element_type=F32)


def _cparams(sem):
    return pltpu.CompilerParams(dimension_semantics=sem, vmem_limit_bytes=VMEM_LIMIT)


def _topk_axis0(v, k, code):
    big = 3.0e38
    out = []
    for _ in range(k):
        m = jnp.max(v, axis=0, keepdims=True)
        c = jnp.min(jnp.where(v == m, code, big), axis=0, keepdims=True)
        out.append((m, c))
        v = jnp.where(code == c, NEG, v)
    return out


def _proj_kernel(x_ref, g_ref, w_ref, obf_ref, vt_ref, of_ref, *, n_bf, n_vt, tk):
    x = x_ref[...]
    ms = jnp.mean(x * x, axis=-1, keepdims=True)
    a = ((x * lax.rsqrt(ms + EPS)) * g_ref[...]).astype(BF16)
    for c0 in range(0, n_bf, 4 * LANES):
        c1 = min(c0 + 4 * LANES, n_bf)
        obf_ref[:, c0:c1] = _dot(a, w_ref[:, c0:c1]).astype(BF16)
    yvt = _dot(a, w_ref[:, n_bf:n_bf + n_vt]).T
    ones = jnp.ones((VT_ROWS - HEAD_DIM, tk), BF16)
    for s in range(vt_ref.shape[0]):
        for c in range(n_vt // HEAD_DIM):
            vt_ref[s, c * VT_ROWS:c * VT_ROWS + HEAD_DIM, :] = (
                yvt[c * HEAD_DIM:(c + 1) * HEAD_DIM, s * tk:(s + 1) * tk].astype(BF16))
            vt_ref[s, c * VT_ROWS + HEAD_DIM:(c + 1) * VT_ROWS, :] = ones
    of_ref[...] = _dot(a, w_ref[:, n_bf + n_vt:])


def _proj(x2, g, w_all, n_bf, n_vt, tm, tk):
    bt, d = x2.shape
    n_all = w_all.shape[1]
    n_f = n_all - n_bf - n_vt
    vt_rows = n_vt // HEAD_DIM * VT_ROWS
    return pl.pallas_call(
        functools.partial(_proj_kernel, n_bf=n_bf, n_vt=n_vt, tk=tk),
        grid=(bt // tm,),
        in_specs=[pl.BlockSpec((tm, d), lambda i: (i, 0)),
                  pl.BlockSpec((1, d), lambda i: (0, 0)),
                  pl.BlockSpec((d, n_all), lambda i: (0, 0))],
        out_specs=[pl.BlockSpec((tm, n_bf), lambda i: (i, 0)),
                   pl.BlockSpec((tm // tk, vt_rows, tk), lambda i: (i, 0, 0)),
                   pl.BlockSpec((tm, n_f), lambda i: (i, 0))],
        out_shape=[jax.ShapeDtypeStruct((bt, n_bf), BF16),
                   jax.ShapeDtypeStruct((bt // tk, vt_rows, tk), BF16),
                   jax.ShapeDtypeStruct((bt, n_f), F32)],
        compiler_params=_cparams(("parallel",)),
        name="proj",
    )(x2, g, w_all)


def _compress_kernel(c_ref, pe_ref, w1_ref, w2_ref, o_ref, ot_ref, *, ncp):
    half = NSA_CMP_STRIDE * HEAD_DIM
    acc = None
    for g in range(NSA_KV):
        c = c_ref[g]
        xa = (c + pe_ref[0:1, :]).astype(BF16)
        xb = (c + pe_ref[1:2, :]).astype(BF16)
        ya = _dot(xa, w1_ref[:half, :])
        yb = _dot(xb, w1_ref[half:, :])
        hid = ya + pltpu.roll(yb, ncp - 1, 0)
        act = jax.nn.gelu(hid).astype(BF16)
        t = _dot(act, w2_ref[g])
        acc = t if acc is None else acc + t
    o_ref[...] = acc.astype(BF16)
    ot_ref[...] = acc.T.astype(BF16)


def _compress(c_all, pe_all, w1_all, w2e):
    _, b, _, ncp, half = c_all.shape
    return pl.pallas_call(
        functools.partial(_compress_kernel, ncp=ncp),
        grid=(2, b),
        in_specs=[pl.BlockSpec((None, None, NSA_KV, ncp, half), lambda s, i: (s, i, 0, 0, 0)),
                  pl.BlockSpec((None, 2, half), lambda s, i: (s, 0, 0)),
                  pl.BlockSpec((None, 2 * half, NSA_CMP_HIDDEN), lambda s, i: (s, 0, 0)),
                  pl.BlockSpec((None, NSA_KV, NSA_CMP_HIDDEN, LANES), lambda s, i: (s, 0, 0, 0))],
        out_specs=[pl.BlockSpec((None, None, ncp, LANES), lambda s, i: (s, i, 0, 0)),
                   pl.BlockSpec((None, None, LANES, ncp), lambda s, i: (s, i, 0, 0))],
        out_shape=[jax.ShapeDtypeStruct((2, b, ncp, LANES), BF16),
                   jax.ShapeDtypeStruct((2, b, LANES, ncp), BF16)],
        compiler_params=_cparams(("parallel", "parallel")),
        name="compress",
    )(c_all, pe_all, w1_all, w2e)


def _nsa_kernel(slopes_ref, q_ref, k_ref, vt_ref, kc_ref, vct_ref, gate_ref, ovt_ref, grp_ref, o_ref,
                qt_scr, oc_scr, bias_scr, m_scr, acc_scr, *, tq, tk, seq, ncp):
    i = pl.program_id(1)
    g = pl.program_id(2)
    q0 = i * tq
    n_sel = seq // NSA_SEL_BLOCK
    n_cmp = ncp - 1
    slope = [slopes_ref[SWA_HEADS + NSA_GROUP * g + hh] * LOG2E for hh in range(NSA_GROUP)]

    kc = kc_ref[...]
    vct = vct_ref[...]
    n_c = lax.broadcasted_iota(jnp.int32, (ncp, tq), 0)
    t_c = q0 + lax.broadcasted_iota(jnp.int32, (ncp, tq), 1)
    dist_c = (t_c - (n_c * NSA_CMP_STRIDE + (NSA_CMP_LEN - 1))).astype(F32)
    valid_c = (dist_c >= 0.0) & (n_c < n_cmp)
    psum = jnp.zeros((ncp, tq), F32)
    qts = []
    for hh in range(NSA_GROUP):
        qts.append(q_ref[:, hh * LANES:(hh + 1) * LANES].astype(F32).T.astype(BF16))
        qt_scr[hh] = qts[hh]
    scs = [_dot(kc, qt) for qt in qts]
    pcs = []
    for hh in range(NSA_GROUP):
        s = jnp.where(valid_c, scs[hh] - slope[hh] * dist_c, NEG)
        m = jnp.max(s, axis=0, keepdims=True)
        e = jnp.where(valid_c, jnp.exp2(s - m), 0.0)
        den = jnp.maximum(jnp.sum(e, axis=0, keepdims=True), 1e-30)
        p = e * (1.0 / den)
        psum = psum + p
        pcs.append(p.astype(BF16))
    for hh in range(NSA_GROUP):
        oc_scr[hh] = _dot(vct, pcs[hh])

    imp = _dot(ovt_ref[...], psum.astype(BF16))
    jb = lax.broadcasted_iota(jnp.int32, (n_sel, tq), 0)
    blk_t = (q0 + lax.broadcasted_iota(jnp.int32, (n_sel, tq), 1)) // NSA_SEL_BLOCK
    imp = jnp.where((jb == 0) | (jb == blk_t) | (jb == blk_t - 1), FORCE_SCORE, imp)
    imp = jnp.where(jb <= blk_t, imp, NEG)
    jbf = jb.astype(F32)
    sel_t = jnp.zeros((n_sel, tq), F32)
    for m_r, idx_r in _topk_axis0(imp, min(NSA_TOPN, n_sel), jbf):
        sel_t = jnp.where((jbf == idx_r) & (m_r > 0.5 * NEG), 1.0, sel_t)
    sel = sel_t.astype(BF16)
    nkt = grp_ref.shape[0]
    cnt = jnp.max(_dot(grp_ref[...], sel), axis=1, keepdims=True)
    tid = lax.broadcasted_iota(jnp.int32, (nkt, 1), 0)
    j_sel = jnp.min(jnp.where((cnt > 0.5) & (tid > 0), tid.astype(F32), float(nkt))).astype(jnp.int32)

    m_scr[...] = jnp.full(m_scr.shape, NEG, F32)
    acc_scr[...] = jnp.zeros(acc_scr.shape, F32)
    rel = (lax.broadcasted_iota(jnp.int32, (tk, tq), 1)
           - lax.broadcasted_iota(jnp.int32, (tk, tq), 0)).astype(F32)
    for hh in range(NSA_GROUP):
        bias_scr[hh] = slope[hh] * rel
    e_rel = (lax.broadcasted_iota(jnp.int32, (tk, n_sel), 1)
             - lax.broadcasted_iota(jnp.int32, (tk, n_sel), 0) // NSA_SEL_BLOCK)

    def update_all(br, k_tile, valid, vt, off):
        qks = [_dot(k_tile, qt_scr[hh]) for hh in range(NSA_GROUP)]
        ps, alphas = [], []
        for hh in range(NSA_GROUP):
            r = br * NSA_GROUP + hh
            shift = slope[hh] * off
            s = jnp.where(valid, qks[hh] - bias_scr[hh], NEG)
            m_old = m_scr[r:r + 1, :]
            m_new = jnp.maximum(m_old, jnp.max(s, axis=0, keepdims=True) - shift)
            alphas.append(jnp.exp2(m_old - m_new))
            ps.append(jnp.exp2(s - (m_new + shift)).astype(BF16))
            m_scr[r:r + 1, :] = m_new
        for hh in range(NSA_GROUP):
            acc_scr[br, hh] = alphas[hh] * acc_scr[br, hh] + _dot(vt, ps[hh])

    vrow = pl.multiple_of(g * VT_ROWS, BF16_ROWS)

    def selected_tile(j):
        k0 = j * tk if isinstance(j, int) else pl.multiple_of(j * tk, tk)
        off = (q0 - k0).astype(F32)
        expand = (e_rel == k0 // NSA_SEL_BLOCK).astype(BF16)
        valid_s = (_dot(expand, sel) > 0.5) & (rel + off >= 0.0)
        update_all(0, k_ref[pl.ds(k0, tk), 0:LANES], valid_s, vt_ref[j, pl.ds(vrow, VT_ROWS), :], off)

    def window_tile(j):
        k0 = pl.multiple_of(j * tk, tk)
        off = (q0 - k0).astype(F32)
        dist = rel + off
        valid_w = (dist >= 0.0) & (dist < float(NSA_WINDOW))
        update_all(1, k_ref[pl.ds(k0, tk), LANES:2 * LANES], valid_w,
                   vt_ref[j, pl.ds(NSA_KV * VT_ROWS + vrow, VT_ROWS), :], off)

    j_win = jnp.maximum(q0 - (NSA_WINDOW - 1), 0) // tk
    j_lo = jnp.minimum(j_sel, j_win)

    @pl.when(j_lo > 0)
    def _():
        selected_tile(0)

    def body(j, carry):
        selected_tile(j)

        @pl.when(j >= j_win)
        def _():
            window_tile(j)

        return carry

    lax.fori_loop(j_lo, (q0 + tq + tk - 1) // tk, body, 0)

    gst = jax.nn.sigmoid(gate_ref[...]).T
    r0 = pl.multiple_of(g * HEAD_DIM, HEAD_DIM)
    heads = []
    for hh in range(NSA_GROUP):
        inv_s = 1.0 / acc_scr[0, hh, HEAD_DIM:HEAD_DIM + 1, :]
        inv_w = 1.0 / acc_scr[1, hh, HEAD_DIM:HEAD_DIM + 1, :]
        heads.append(gst[3 * hh:3 * hh + 1, :] * oc_scr[hh, pl.ds(r0, HEAD_DIM), :]
                     + (gst[3 * hh + 1:3 * hh + 2, :] * inv_s) * acc_scr[0, hh, 0:HEAD_DIM, :]
                     + (gst[3 * hh + 2:3 * hh + 3, :] * inv_w) * acc_scr[1, hh, 0:HEAD_DIM, :])
    o_ref[...] = jnp.concatenate(heads, axis=0).T.astype(BF16)


def _nsa(slopes, pbf, vt3, pf, kcmp, kcmpt, ovt, grp, batch, seq, tq, tk, k_col):
    nq = seq // tq
    ncp = kcmp.shape[2]
    n_sel = seq // NSA_SEL_BLOCK
    nkt = seq // tk
    gw = NSA_GROUP * LANES
    return pl.pallas_call(
        functools.partial(_nsa_kernel, tq=tq, tk=tk, seq=seq, ncp=ncp),
        grid=(batch, nq, NSA_KV),
        in_specs=[pl.BlockSpec(memory_space=pltpu.SMEM),
                  pl.BlockSpec((tq, gw), lambda b, i, g: (b * nq + i, g)),
                  pl.BlockSpec((seq, 2 * LANES), lambda b, i, g: (b, k_col // (2 * LANES))),
                  pl.BlockSpec((nkt, vt3.shape[1], tk), lambda b, i, g: (b, 0, 0)),
                  pl.BlockSpec((None, None, ncp, LANES), lambda b, i, g: (0, b, 0, 0)),
                  pl.BlockSpec((None, None, LANES, ncp), lambda b, i, g: (1, b, 0, 0)),
                  pl.BlockSpec((tq, LANES), lambda b, i, g: (b * nq + i, 2 + g)),
                  pl.BlockSpec((n_sel, ncp), lambda b, i, g: (0, 0)),
                  pl.BlockSpec((nkt, n_sel), lambda b, i, g: (0, 0))],
        out_specs=pl.BlockSpec((tq, NSA_GROUP * HEAD_DIM), lambda b, i, g: (b * nq + i, g)),
        out_shape=jax.ShapeDtypeStruct((batch * seq, NSA_HEADS * HEAD_DIM), BF16),
        scratch_shapes=[pltpu.VMEM((NSA_GROUP, LANES, tq), BF16),
                        pltpu.VMEM((NSA_GROUP, LANES, tq), F32),
                        pltpu.VMEM((NSA_GROUP, tk, tq), F32),
                        pltpu.VMEM((2 * NSA_GROUP, tq), F32),
                        pltpu.VMEM((2, NSA_GROUP, VT_ROWS, tq), F32)],
        compiler_params=_cparams(("parallel", "parallel", "arbitrary")),
        name="nsa",
    )(slopes, pbf, pbf, vt3, kcmp, kcmpt, pf, ovt, grp)


def _swa_kernel(slopes_ref, sinks_ref, q_ref, kv_ref, o_ref, *, tq):
    i = pl.program_id(1)
    q0 = i * tq
    kwid = tq + SWA_WINDOW
    start = pl.multiple_of(jnp.maximum(q0 - SWA_WINDOW, 0), SWA_WINDOW)
    kvw = kv_ref[pl.ds(start, kwid), :]
    dist = ((q0 - start) + lax.broadcasted_iota(jnp.int32, (tq, kwid), 0)
            - lax.broadcasted_iota(jnp.int32, (tq, kwid), 1)).astype(F32)
    valid = (dist >= 0.0) & (dist < float(SWA_WINDOW))
    lo_half = lax.broadcasted_iota(jnp.int32, (tq, LANES), 1) < HEAD_DIM
    qks = [_dot_nt(q_ref[:, h * LANES:(h + 1) * LANES], kvw) for h in range(SWA_HEADS)]
    es, dens = [], []
    for h in range(SWA_HEADS):
        s = jnp.where(valid, qks[h] - slopes_ref[h] * dist, NEG)
        sink = sinks_ref[h]
        m = jnp.maximum(jnp.max(s, axis=1, keepdims=True), sink)
        e = jnp.where(valid, jnp.exp(s - m), 0.0)
        dens.append(jnp.sum(e, axis=1, keepdims=True) + jnp.exp(sink - m))
        es.append(e.astype(BF16))
    outs = [_dot(es[h], kvw) / dens[h] for h in range(SWA_HEADS)]
    for pr in range(SWA_HEADS // 2):
        a_lo = pltpu.roll(outs[2 * pr], HEAD_DIM, 1)
        o_ref[:, pr * LANES:(pr + 1) * LANES] = jnp.where(lo_half, a_lo, outs[2 * pr + 1]).astype(BF16)


def _swa(slopes, sinks, pbf, batch, seq, tq, q_col, kv_col):
    nq = seq // tq
    qw = SWA_HEADS * LANES
    return pl.pallas_call(
        functools.partial(_swa_kernel, tq=tq),
        grid=(batch, nq),
        in_specs=[pl.BlockSpec(memory_space=pltpu.SMEM),
                  pl.BlockSpec(memory_space=pltpu.SMEM),
                  pl.BlockSpec((tq, qw), lambda b, i: (b * nq + i, q_col // qw)),
                  pl.BlockSpec((seq, LANES), lambda b, i: (b, kv_col // LANES))],
        out_specs=pl.BlockSpec((tq, SWA_HEADS * HEAD_DIM), lambda b, i: (b * nq + i, 0)),
        out_shape=jax.ShapeDtypeStruct((batch * seq, SWA_HEADS * HEAD_DIM), BF16),
        compiler_params=_cparams(("parallel", "parallel")),
        name="swa",
    )(slopes, sinks, pbf, pbf)


def _mid_kernel(x_ref, on_ref, os_ref, wo_ref, g2_ref, wq_ref, keys_ref,
                h_ref, h2_ref, st_ref):
    half = on_ref.shape[1]
    h = x_ref[...] + _dot(on_ref[...], wo_ref[:half, :]) + _dot(os_ref[...], wo_ref[half:, :])
    h_ref[...] = h
    ms = jnp.mean(h * h, axis=-1, keepdims=True)
    h2 = ((h * lax.rsqrt(ms + EPS)) * g2_ref[...]).astype(BF16)
    h2_ref[...] = h2
    qhs = [_dot(h2, wq_ref[:, 2 * hp * LANES:2 * (hp + 1) * LANES]).astype(BF16)
           for hp in range(PEER_HEADS)]
    for c in range(2 * PEER_HEADS):
        st_ref[c] = _dot_nt(keys_ref[c], qhs[c // 2][:, (c % 2) * LANES:(c % 2 + 1) * LANES])


def _mid(x2, o_n, o_s, wo, g2, wq, keys, tm):
    bt, d = x2.shape
    nk = 2 * PEER_HEADS
    return pl.pallas_call(
        _mid_kernel,
        grid=(bt // tm,),
        in_specs=[pl.BlockSpec((tm, d), lambda i: (i, 0)),
                  pl.BlockSpec((tm, o_n.shape[1]), lambda i: (i, 0)),
                  pl.BlockSpec((tm, o_s.shape[1]), lambda i: (i, 0)),
                  pl.BlockSpec(wo.shape, lambda i: (0, 0)),
                  pl.BlockSpec((1, d), lambda i: (0, 0)),
                  pl.BlockSpec(wq.shape, lambda i: (0, 0)),
                  pl.BlockSpec(keys.shape, lambda i: (0, 0, 0))],
        out_specs=[pl.BlockSpec((tm, d), lambda i: (i, 0)),
                   pl.BlockSpec((tm, d), lambda i: (i, 0)),
                   pl.BlockSpec((nk, PEER_NKEYS, tm), lambda i: (0, 0, i))],
        out_shape=[jax.ShapeDtypeStruct((bt, d), F32),
                   jax.ShapeDtypeStruct((bt, d), BF16),
                   jax.ShapeDtypeStruct((nk, PEER_NKEYS, bt), F32)],
        compiler_params=_cparams(("parallel",)),
        name="mid",
    )(x2, o_n, o_s, wo, g2, wq, keys)


def _pair_groups(k):
    groups = []
    i = 0
    while k // (i + 1) > 1:
        n = k // (i + 1)
        groups.append((i, n, -(-n // SUBLANES) * SUBLANES))
        i += 1
    return groups, i


def _peer_topk_kernel(st_ref, a_ref, b_ref, g_ref,
                      v_scr, i_scr, cand_scr, c_scr, f_scr, sa_scr, sb_scr, sg_scr):
    k = PEER_TOPK
    tt = st_ref.shape[2]
    groups, tail = _pair_groups(k)
    rid128 = lax.broadcasted_iota(jnp.int32, (PEER_NKEYS, tt), 0).astype(F32)
    codes, pads = [], []
    for (i, n, rows) in groups:
        j = lax.broadcasted_iota(jnp.int32, (rows, tt), 0)
        codes.append((j + i * k).astype(F32))
        pads.append(j < n)
    jt = lax.broadcasted_iota(jnp.int32, (k - tail, tt), 0)
    codes.append(((jt + tail) * k).astype(F32))
    code = jnp.concatenate(codes, axis=0)

    for h in range(PEER_HEADS):
        for c in range(2):
            for r, (m, idx) in enumerate(_topk_axis0(st_ref[2 * h + c], k, rid128)):
                v_scr[c, r:r + 1, :] = m
                i_scr[c, r:r + 1, :] = idx
        row = 0
        for gi, (i, n, rows) in enumerate(groups):
            vals = v_scr[0, i:i + 1, :] + v_scr[1, 0:rows, :]
            cand_scr[row:row + rows, :] = jnp.where(pads[gi], vals, NEG)
            row += rows
        cand_scr[row:row + k - tail, :] = v_scr[0, tail:k, :] + v_scr[1, 0:1, :]
        for r, (m, f) in enumerate(_topk_axis0(cand_scr[...], k, code)):
            c_scr[r:r + 1, :] = m
            f_scr[r:r + 1, :] = f
        cs, fl = c_scr[...], f_scr[...]
        fi = jnp.floor(fl * (1.0 / k))
        fj = fl - fi * k
        a = jnp.zeros_like(fl)
        b = jnp.zeros_like(fl)
        for r in range(k):
            a = jnp.where(fi == float(r), i_scr[0, r:r + 1, :], a)
            b = jnp.where(fj == float(r), i_scr[1, r:r + 1, :], b)
        e = jnp.exp(cs - cs[0:1, :])
        sa_scr[h * k:(h + 1) * k, :] = a
        sb_scr[h * k:(h + 1) * k, :] = b
        sg_scr[h * k:(h + 1) * k, :] = e / jnp.sum(e, axis=0, keepdims=True)
    a_ref[...] = sa_scr[...].T
    b_ref[...] = sb_scr[...].T
    g_ref[...] = sg_scr[...].T


def _peer_topk(st, tt):
    nk, _, bt = st.shape
    k = PEER_TOPK
    ns = PEER_HEADS * k
    groups, tail = _pair_groups(k)
    n_cand = sum(rows for _, _, rows in groups) + k - tail
    out = jax.ShapeDtypeStruct((bt, ns), F32)
    spec = pl.BlockSpec((tt, ns), lambda i: (i, 0))
    return pl.pallas_call(
        _peer_topk_kernel,
        grid=(bt // tt,),
        in_specs=[pl.BlockSpec((nk, PEER_NKEYS, tt), lambda i: (0, 0, i))],
        out_specs=[spec, spec, spec],
        out_shape=[out, out, out],
        scratch_shapes=[pltpu.VMEM((2, k, tt), F32), pltpu.VMEM((2, k, tt), F32),
                        pltpu.VMEM((n_cand, tt), F32),
                        pltpu.VMEM((k, tt), F32), pltpu.VMEM((k, tt), F32),
                        pltpu.VMEM((ns, tt), F32), pltpu.VMEM((ns, tt), F32),
                        pltpu.VMEM((ns, tt), F32)],
        compiler_params=_cparams(("parallel",)),
        name="peer_topk",
    )(st)


def _peer_w_kernel(a_ref, b_ref, g_ref, w_ref, s_scr, *, tt, pitch, unroll):
    nk = PEER_NKEYS
    sub = lax.broadcasted_iota(jnp.int32, (nk, a_ref.shape[1]), 0).astype(F32)

    def per_token(t, carry):
        a_row = a_ref[pl.ds(t, 1), :]
        b_row = b_ref[pl.ds(t, 1), :]
        g_row = g_ref[pl.ds(t, 1), :]
        x = jnp.where(a_row == sub, g_row, 0.0).astype(BF16)
        y = jnp.where(b_row == sub, 1.0, 0.0).astype(BF16)
        s_scr[pl.ds(t, nk, stride=pitch), :] = _dot_nt(x, y)
        return carry

    lax.fori_loop(0, tt, per_token, 0, unroll=unroll)

    def per_block(i1, carry):
        r0 = pl.multiple_of(i1 * pitch, SUBLANES)
        w_ref[i1] = s_scr[pl.ds(r0, tt), :].astype(BF16)
        return carry

    lax.fori_loop(0, nk, per_block, 0, unroll=unroll)


def _peer_w(a, b, g, tt):
    bt, ns = a.shape
    nk = PEER_NKEYS
    pitch = tt + SUBLANES
    spec = pl.BlockSpec((tt, ns), lambda i: (i, 0))
    return pl.pallas_call(
        functools.partial(_peer_w_kernel, tt=tt, pitch=pitch, unroll=8),
        grid=(bt // tt,),
        in_specs=[spec, spec, spec],
        out_specs=pl.BlockSpec((nk, tt, nk), lambda i: (0, i, 0)),
        out_shape=jax.ShapeDtypeStruct((nk, bt, nk), BF16),
        scratch_shapes=[pltpu.VMEM((nk * pitch, nk), F32)],
        compiler_params=_cparams(("parallel",)),
        name="peer_w",
    )(a, b, g)


def _peer_ffn_kernel(h2_ref, ue_ref, uo_ref, ve_ref, vo_ref, we_ref, wo_ref, h_ref, gf_ref, o_ref,
                     acc_scr, act0, act1):
    j = pl.program_id(1)
    last = pl.num_programs(1) - 1

    @pl.when(j == 0)
    def _():
        acc_scr[...] = jnp.zeros(acc_scr.shape, F32)
        act1[...] = jnp.zeros(act1.shape, F32)

    def consume(act_ref, w_ref, v_ref):
        z = [(jax.nn.gelu(act_ref[:, c * LANES:(c + 1) * LANES]) * w_ref[c].astype(F32)).astype(BF16)
             for c in range(w_ref.shape[0])]
        acc_scr[...] += _dot(jnp.concatenate(z, axis=1), v_ref[...])

    h2 = h2_ref[...]
    act0[...] = _dot_nt(h2, ue_ref[...])
    consume(act1, we_ref, ve_ref)

    @pl.when(j < last)
    def _():
        act1[...] = _dot_nt(h2, uo_ref[...])
        consume(act0, wo_ref, vo_ref)

    @pl.when(j == last)
    def _():
        h = h_ref[...] + acc_scr[...]
        ms = jnp.mean(h * h, axis=-1, keepdims=True)
        o_ref[...] = (h * lax.rsqrt(ms + EPS)) * gf_ref[...]


def _peer_ffn(h2, u, v, w, h, gf, tm, te):
    bt, d = h.shape
    ne = u.shape[0] // te
    assert ne % 2 == 0
    nb = te // LANES
    top = ne - 1
    produce_e = lambda i, j: (jnp.minimum(2 * j, top), 0)
    produce_o = lambda i, j: (jnp.minimum(2 * j + 1, top), 0)
    consume_e = lambda i, j: (jnp.maximum(2 * j - 1, 0), 0)
    consume_o = lambda i, j: (jnp.minimum(2 * j, top), 0)
    return pl.pallas_call(
        _peer_ffn_kernel,
        grid=(bt // tm, ne // 2 + 1),
        in_specs=[pl.BlockSpec((tm, d), lambda i, j: (i, 0)),
                  pl.BlockSpec((te, d), produce_e),
                  pl.BlockSpec((te, d), produce_o),
                  pl.BlockSpec((te, d), consume_e),
                  pl.BlockSpec((te, d), consume_o),
                  pl.BlockSpec((nb, tm, LANES), lambda i, j: (jnp.maximum(2 * j - 1, 0), i, 0)),
                  pl.BlockSpec((nb, tm, LANES), lambda i, j: (jnp.minimum(2 * j, top), i, 0)),
                  pl.BlockSpec((tm, d), lambda i, j: (i, 0)),
                  pl.BlockSpec((1, d), lambda i, j: (0, 0))],
        out_specs=pl.BlockSpec((tm, d), lambda i, j: (i, 0)),
        out_shape=jax.ShapeDtypeStruct((bt, d), F32),
        scratch_shapes=[pltpu.VMEM((tm, d), F32), pltpu.VMEM((tm, te), F32), pltpu.VMEM((tm, te), F32)],
        compiler_params=_cparams(("parallel", "arbitrary")),
        name="peer_ffn",
    )(h2, u, u, v, v, w, w, h, gf)


def _split_offsets():
    hd = HEAD_DIM
    sizes = [NSA_HEADS * hd, NSA_KV * hd, NSA_KV * hd, NSA_KV * hd, NSA_KV * hd, NSA_KV * hd,
             NSA_KV * hd, NSA_HEADS * 3, SWA_HEADS * hd, hd, hd]
    return [0] + [int(c) for c in np.cumsum(sizes)]


def _pick(n, prefs):
    for p in prefs:
        if n % p == 0:
            return p
    return n


def kernel(x, ln1_g, w_in, cmp_pe_k, cmp_w1_k, cmp_w2_k, cmp_pe_v, cmp_w1_v, cmp_w2_v, swa_sinks, w_out, ln2_g, peer_wq, peer_keys, peer_u, peer_v, lnf_g):
    batch, seq, d = x.shape
    bt = batch * seq
    hd = HEAD_DIM
    nsa_tq, nsa_tk = 256, 256
    assert ln1_g.shape[0] == 1, "single layer"
    assert seq % 512 == 0
    slopes = jnp.asarray((2.0 ** (-8.0 * (np.arange(N_HEADS) + 1) / N_HEADS)).astype(np.float32))

    off = _split_offsets()
    w = w_in[0]
    col = lambda k: w[:, off[k]:off[k + 1]]
    scale = hd ** -0.5
    z64 = jnp.zeros((d, hd), F32)
    qn, qs = col(0) * (scale * LOG2E), col(8) * scale
    qn_exp = []
    for h in range(NSA_HEADS):
        qh = qn[:, h * hd:(h + 1) * hd]
        qn_exp += [qh, z64] if h // NSA_GROUP == 0 else [z64, qh]
    qs_exp = []
    for h in range(SWA_HEADS):
        qs_exp += [qs[:, h * hd:(h + 1) * hd], z64]
    gt = col(7)
    gpad = jnp.zeros((d, LANES - NSA_GROUP * 3), F32)
    gcols = []
    for g in range(NSA_KV):
        gcols += [gt[:, g * NSA_GROUP * 3:(g + 1) * NSA_GROUP * 3], gpad]
    w_all = jnp.concatenate(qn_exp + qs_exp + [col(3), col(5), col(9), col(10)]
                            + [col(4), col(6)] + [col(1), col(2)] + gcols, axis=1).astype(BF16)
    swa_q_col = NSA_HEADS * LANES
    nsa_k_col = swa_q_col + SWA_HEADS * LANES
    swa_kv_col = nsa_k_col + 2 * LANES
    n_bf = swa_kv_col + LANES
    n_vt = 2 * LANES

    x2 = x.reshape(bt, d)
    pbf, vt3, pf = _proj(x2, ln1_g[0][None, :], w_all, n_bf, n_vt, 512, nsa_tk)

    ncp = seq // NSA_CMP_STRIDE
    c_all = pf[:, :2 * LANES].reshape(batch, ncp, NSA_CMP_STRIDE, 2, NSA_KV, hd)
    c_all = c_all.transpose(3, 0, 4, 1, 2, 5).reshape(2, batch, NSA_KV, ncp, NSA_CMP_STRIDE * hd)
    pe_all = jnp.stack([cmp_pe_k[0].reshape(2, -1), cmp_pe_v[0].reshape(2, -1)])
    w1_all = jnp.stack([cmp_w1_k[0], cmp_w1_v[0]]).astype(BF16)
    zc = jnp.zeros((NSA_CMP_HIDDEN, hd), F32)
    w2e = jnp.stack([jnp.stack([jnp.concatenate([w2, zc], axis=1), jnp.concatenate([zc, w2], axis=1)])
                     for w2 in (cmp_w2_k[0], cmp_w2_v[0])]).astype(BF16)
    kcmp, kcmpt = _compress(c_all, pe_all, w1_all, w2e)

    n_sel = seq // NSA_SEL_BLOCK
    c0 = np.arange(ncp)[None, :] * NSA_CMP_STRIDE
    s0 = np.arange(n_sel)[:, None] * NSA_SEL_BLOCK
    ovt = ((c0 < s0 + NSA_SEL_BLOCK) & (c0 + NSA_CMP_LEN > s0) & (np.arange(ncp)[None, :] < ncp - 1))
    ovt = jnp.asarray(ovt.astype(np.float32), BF16)
    grp = (np.arange(n_sel)[None, :] * NSA_SEL_BLOCK // nsa_tk) == np.arange(seq // nsa_tk)[:, None]
    grp = jnp.asarray(grp.astype(np.float32), BF16)

    o_n = _nsa(slopes, pbf, vt3, pf, kcmp, kcmpt, ovt, grp, batch, seq, nsa_tq, nsa_tk, nsa_k_col)
    o_s = _swa(slopes, swa_sinks[0], pbf, batch, seq, 128, swa_q_col, swa_kv_col)

    keys = peer_keys[0].reshape(2 * PEER_HEADS, PEER_NKEYS, -1).astype(BF16)
    h, h2, st = _mid(x2, o_n, o_s, w_out[0].astype(BF16), ln2_g[0][None, :],
                     peer_wq[0].astype(BF16), keys, _pick(bt, (256, 128)))
    a, b, g = _peer_topk(st, 128)
    wd = _peer_w(a, b, g, 128)
    out = _peer_ffn(h2, peer_u[0].astype(BF16), peer_v[0].astype(BF16), wd, h, lnf_g[None, :],
                    _pick(bt, (1024, 512, 256, 128)), 512)
    return out.reshape(batch, seq, d)
```

```python
import functools

import numpy as np
import jax
import jax.numpy as jnp
from jax import lax
from jax.experimental import pallas as pl
from jax.experimental.pallas import tpu as pltpu

F32 = jnp.float32
BF16 = jnp.bfloat16

HEAD_DIM = 64
N_HEADS = 16
NSA_HEADS = 8
NSA_KV = 2
NSA_GROUP = 4
SWA_HEADS = 8
NSA_CMP_LEN = 32
NSA_CMP_STRIDE = 16
NSA_CMP_HIDDEN = 256
NSA_SEL_BLOCK = 64
NSA_TOPN = 16
NSA_WINDOW = 512
SWA_WINDOW = 128
FORCE_SCORE = 1.0e4
PEER_HEADS = 8
PEER_NKEYS = 128
PEER_TOPK = 16
EPS = 1e-6
NEG = -1.0e30
LANES = 128
SUBLANES = 8
BF16_ROWS = 16
VT_ROWS = HEAD_DIM + BF16_ROWS
LOG2E = 1.4426950408889634
VMEM_LIMIT = 56 * 1024 * 1024


def _dot(a, b):
    return jnp.dot(a, b, preferred_element_type=F32)


def _dot_nt(a, b):
    return lax.dot_general(a, b, (((1,), (1,)), ((), ())), preferred_element_type=F32)


def _cparams(sem):
    return pltpu.CompilerParams(dimension_semantics=sem, vmem_limit_bytes=VMEM_LIMIT)


def _topk_axis0(v, k, code):
    big = 3.0e38
    out = []
    for _ in range(k):
        m = jnp.max(v, axis=0, keepdims=True)
        c = jnp.min(jnp.where(v == m, code, big), axis=0, keepdims=True)
        out.append((m, c))
        v = jnp.where(code == c, NEG, v)
    return out


def _proj_kernel(x_ref, g_ref, w_ref, obf_ref, vt_ref, of_ref, *, n_bf, n_vt, tk):
    x = x_ref[...]
    ms = jnp.mean(x * x, axis=-1, keepdims=True)
    a = ((x * lax.rsqrt(ms + EPS)) * g_ref[...]).astype(BF16)
    for c0 in range(0, n_bf, 4 * LANES):
        c1 = min(c0 + 4 * LANES, n_bf)
        obf_ref[:, c0:c1] = _dot(a, w_ref[:, c0:c1]).astype(BF16)
    yvt = _dot(a, w_ref[:, n_bf:n_bf + n_vt]).T
    ones = jnp.ones((VT_ROWS - HEAD_DIM, tk), BF16)
    for s in range(vt_ref.shape[0]):
        for c in range(n_vt // HEAD_DIM):
            vt_ref[s, c * VT_ROWS:c * VT_ROWS + HEAD_DIM, :] = (
                yvt[c * HEAD_DIM:(c + 1) * HEAD_DIM, s * tk:(s + 1) * tk].astype(BF16))
            vt_ref[s, c * VT_ROWS + HEAD_DIM:(c + 1) * VT_ROWS, :] = ones
    of_ref[...] = _dot(a, w_ref[:, n_bf + n_vt:])


def _proj(x2, g, w_all, n_bf, n_vt, tm, tk):
    bt, d = x2.shape
    n_all = w_all.shape[1]
    n_f = n_all - n_bf - n_vt
    vt_rows = n_vt // HEAD_DIM * VT_ROWS
    return pl.pallas_call(
        functools.partial(_proj_kernel, n_bf=n_bf, n_vt=n_vt, tk=tk),
        grid=(bt // tm,),
        in_specs=[pl.BlockSpec((tm, d), lambda i: (i, 0)),
                  pl.BlockSpec((1, d), lambda i: (0, 0)),
                  pl.BlockSpec((d, n_all), lambda i: (0, 0))],
        out_specs=[pl.BlockSpec((tm, n_bf), lambda i: (i, 0)),
                   pl.BlockSpec((tm // tk, vt_rows, tk), lambda i: (i, 0, 0)),
                   pl.BlockSpec((tm, n_f), lambda i: (i, 0))],
        out_shape=[jax.ShapeDtypeStruct((bt, n_bf), BF16),
                   jax.ShapeDtypeStruct((bt // tk, vt_rows, tk), BF16),
                   jax.ShapeDtypeStruct((bt, n_f), F32)],
        compiler_params=_cparams(("parallel",)),
        name="proj",
    )(x2, g, w_all)


def _compress_kernel(c_ref, pe_ref, w1_ref, w2_ref, o_ref, ot_ref, *, ncp):
    half = NSA_CMP_STRIDE * HEAD_DIM
    acc = None
    for g in range(NSA_KV):
        c = c_ref[g]
        xa = (c + pe_ref[0:1, :]).astype(BF16)
        xb = (c + pe_ref[1:2, :]).astype(BF16)
        ya = _dot(xa, w1_ref[:half, :])
        yb = _dot(xb, w1_ref[half:, :])
        hid = ya + pltpu.roll(yb, ncp - 1, 0)
        act = jax.nn.gelu(hid).astype(BF16)
        t = _dot(act, w2_ref[g])
        acc = t if acc is None else acc + t
    o_ref[...] = acc.astype(BF16)
    ot_ref[...] = acc.T.astype(BF16)


def _compress(c_all, pe_all, w1_all, w2e):
    _, b, _, ncp, half = c_all.shape
    return pl.pallas_call(
        functools.partial(_compress_kernel, ncp=ncp),
        grid=(2, b),
        in_specs=[pl.BlockSpec((None, None, NSA_KV, ncp, half), lambda s, i: (s, i, 0, 0, 0)),
                  pl.BlockSpec((None, 2, half), lambda s, i: (s, 0, 0)),
                  pl.BlockSpec((None, 2 * half, NSA_CMP_HIDDEN), lambda s, i: (s, 0, 0)),
                  pl.BlockSpec((None, NSA_KV, NSA_CMP_HIDDEN, LANES), lambda s, i: (s, 0, 0, 0))],
        out_specs=[pl.BlockSpec((None, None, ncp, LANES), lambda s, i: (s, i, 0, 0)),
                   pl.BlockSpec((None, None, LANES, ncp), lambda s, i: (s, i, 0, 0))],
        out_shape=[jax.ShapeDtypeStruct((2, b, ncp, LANES), BF16),
                   jax.ShapeDtypeStruct((2, b, LANES, ncp), BF16)],
        compiler_params=_cparams(("parallel", "parallel")),
        name="compress",
    )(c_all, pe_all, w1_all, w2e)


def _nsa_kernel(slopes_ref, q_ref, k_ref, vt_ref, kc_ref, vct_ref, gate_ref, ovt_ref, grp_ref, o_ref,
                qt_scr, oc_scr, bias_scr, m_scr, acc_scr, *, tq, tk, seq, ncp):
    i = pl.program_id(1)
    g = pl.program_id(2)
    q0 = i * tq
    n_sel = seq // NSA_SEL_BLOCK
    n_cmp = ncp - 1
    slope = [slopes_ref[SWA_HEADS + NSA_GROUP * g + hh] * LOG2E for hh in range(NSA_GROUP)]

    kc = kc_ref[...]
    vct = vct_ref[...]
    n_c = lax.broadcasted_iota(jnp.int32, (ncp, tq), 0)
    t_c = q0 + lax.broadcasted_iota(jnp.int32, (ncp, tq), 1)
    dist_c = (t_c - (n_c * NSA_CMP_STRIDE + (NSA_CMP_LEN - 1))).astype(F32)
    valid_c = (dist_c >= 0.0) & (n_c < n_cmp)
    psum = jnp.zeros((ncp, tq), F32)
    qts = []
    for hh in range(NSA_GROUP):
        qts.append(q_ref[:, hh * LANES:(hh + 1) * LANES].astype(F32).T.astype(BF16))
        qt_scr[hh] = qts[hh]
    scs = [_dot(kc, qt) for qt in qts]
    pcs = []
    for hh in range(NSA_GROUP):
        s = jnp.where(valid_c, scs[hh] - slope[hh] * dist_c, NEG)
        m = jnp.max(s, axis=0, keepdims=True)
        e = jnp.where(valid_c, jnp.exp2(s - m), 0.0)
        den = jnp.maximum(jnp.sum(e, axis=0, keepdims=True), 1e-30)
        p = e * (1.0 / den)
        psum = psum + p
        pcs.append(p.astype(BF16))
    for hh in range(NSA_GROUP):
        oc_scr[hh] = _dot(vct, pcs[hh])

    imp = _dot(ovt_ref[...], psum.astype(BF16))
    jb = lax.broadcasted_iota(jnp.int32, (n_sel, tq), 0)
    blk_t = (q0 + lax.broadcasted_iota(jnp.int32, (n_sel, tq), 1)) // NSA_SEL_BLOCK
    forced = ((jb == 0) | (jb == blk_t) | (jb == blk_t - 1)) & (jb <= blk_t)
    imp = jnp.where(forced | (jb > blk_t), NEG, imp)
    jbf = jb.astype(F32)
    sel_t = jnp.where(forced, 1.0, 0.0)
    for m_r, idx_r in _topk_axis0(imp, min(NSA_TOPN, n_sel) - 3, jbf):
        sel_t = jnp.where((jbf == idx_r) & (m_r > 0.5 * NEG), 1.0, sel_t)
    sel = sel_t.astype(BF16)
    nkt = grp_ref.shape[0]
    cnt = jnp.max(_dot(grp_ref[...], sel), axis=1, keepdims=True)
    tid = lax.broadcasted_iota(jnp.int32, (nkt, 1), 0)
    j_sel = jnp.min(jnp.where((cnt > 0.5) & (tid > 0), tid.astype(F32), float(nkt))).astype(jnp.int32)

    m_scr[...] = jnp.full(m_scr.shape, NEG, F32)
    acc_scr[...] = jnp.zeros(acc_scr.shape, F32)
    rel = (lax.broadcasted_iota(jnp.int32, (tk, tq), 1)
           - lax.broadcasted_iota(jnp.int32, (tk, tq), 0)).astype(F32)
    for hh in range(NSA_GROUP):
        bias_scr[hh] = slope[hh] * rel
    e_rel = (lax.broadcasted_iota(jnp.int32, (tk, n_sel), 1)
             - lax.broadcasted_iota(jnp.int32, (tk, n_sel), 0) // NSA_SEL_BLOCK)

    def flash_update(branches, off):
        qks = [[_dot(k_tile, qt_scr[hh]) for hh in range(NSA_GROUP)] for _, k_tile, _, _ in branches]
        ps, alphas = [], []
        for bi, (br, _, valid, _) in enumerate(branches):
            for hh in range(NSA_GROUP):
                r = br * NSA_GROUP + hh
                shift = slope[hh] * off
                s = jnp.where(valid, qks[bi][hh] - bias_scr[hh], NEG)
                m_old = m_scr[r:r + 1, :]
                m_new = jnp.maximum(m_old, jnp.max(s, axis=0, keepdims=True) - shift)
                alphas.append(jnp.exp2(m_old - m_new))
                ps.append(jnp.exp2(s - (m_new + shift)).astype(BF16))
                m_scr[r:r + 1, :] = m_new
        for bi, (br, _, _, vt) in enumerate(branches):
            for hh in range(NSA_GROUP):
                n = bi * NSA_GROUP + hh
                acc_scr[br, hh] = alphas[n] * acc_scr[br, hh] + _dot(vt, ps[n])

    vrow = pl.multiple_of(g * VT_ROWS, BF16_ROWS)

    def tile(j, with_window):
        k0 = j * tk if isinstance(j, int) else pl.multiple_of(j * tk, tk)
        off = (q0 - k0).astype(F32)
        dist = rel + off
        causal = dist >= 0.0
        expand = (e_rel == k0 // NSA_SEL_BLOCK).astype(BF16)
        branches = [(0, k_ref[pl.ds(k0, tk), 0:LANES], (_dot(expand, sel) > 0.5) & causal,
                     vt_ref[j, pl.ds(vrow, VT_ROWS), :])]
        if with_window:
            branches.append((1, k_ref[pl.ds(k0, tk), LANES:2 * LANES], causal & (dist < float(NSA_WINDOW)),
                             vt_ref[j, pl.ds(NSA_KV * VT_ROWS + vrow, VT_ROWS), :]))
        flash_update(branches, off)

    j_win = jnp.maximum(q0 - (NSA_WINDOW - 1), 0) // tk
    j_lo = jnp.minimum(j_sel, j_win)

    @pl.when(j_lo > 0)
    def _():
        tile(0, False)

    def body(j, carry):
        @pl.when(j < j_win)
        def _():
            tile(j, False)

        @pl.when(j >= j_win)
        def _():
            tile(j, True)

        return carry

    lax.fori_loop(j_lo, (q0 + tq + tk - 1) // tk, body, 0)

    gst = jax.nn.sigmoid(gate_ref[...]).T
    r0 = pl.multiple_of(g * HEAD_DIM, HEAD_DIM)
    heads = []
    for hh in range(NSA_GROUP):
        inv_s = 1.0 / acc_scr[0, hh, HEAD_DIM:HEAD_DIM + 1, :]
        inv_w = 1.0 / acc_scr[1, hh, HEAD_DIM:HEAD_DIM + 1, :]
        heads.append(gst[3 * hh:3 * hh + 1, :] * oc_scr[hh, pl.ds(r0, HEAD_DIM), :]
                     + (gst[3 * hh + 1:3 * hh + 2, :] * inv_s) * acc_scr[0, hh, 0:HEAD_DIM, :]
                     + (gst[3 * hh + 2:3 * hh + 3, :] * inv_w) * acc_scr[1, hh, 0:HEAD_DIM, :])
    o_ref[...] = jnp.concatenate(heads, axis=0).T.astype(BF16)


def _nsa(slopes, pbf, vt3, pf, kcmp, kcmpt, ovt, grp, batch, seq, tq, tk, k_col):
    nq = seq // tq
    ncp = kcmp.shape[2]
    n_sel = seq // NSA_SEL_BLOCK
    nkt = seq // tk
    gw = NSA_GROUP * LANES
    return pl.pallas_call(
        functools.partial(_nsa_kernel, tq=tq, tk=tk, seq=seq, ncp=ncp),
        grid=(batch, nq, NSA_KV),
        in_specs=[pl.BlockSpec(memory_space=pltpu.SMEM),
                  pl.BlockSpec((tq, gw), lambda b, i, g: (b * nq + i, g)),
                  pl.BlockSpec((seq, 2 * LANES), lambda b, i, g: (b, k_col // (2 * LANES))),
                  pl.BlockSpec((nkt, vt3.shape[1], tk), lambda b, i, g: (b, 0, 0)),
                  pl.BlockSpec((None, None, ncp, LANES), lambda b, i, g: (0, b, 0, 0)),
                  pl.BlockSpec((None, None, LANES, ncp), lambda b, i, g: (1, b, 0, 0)),
                  pl.BlockSpec((tq, LANES), lambda b, i, g: (b * nq + i, 2 + g)),
                  pl.BlockSpec((n_sel, ncp), lambda b, i, g: (0, 0)),
                  pl.BlockSpec((nkt, n_sel), lambda b, i, g: (0, 0))],
        out_specs=pl.BlockSpec((tq, NSA_GROUP * HEAD_DIM), lambda b, i, g: (b * nq + i, g)),
        out_shape=jax.ShapeDtypeStruct((batch * seq, NSA_HEADS * HEAD_DIM), BF16),
        scratch_shapes=[pltpu.VMEM((NSA_GROUP, LANES, tq), BF16),
                        pltpu.VMEM((NSA_GROUP, LANES, tq), F32),
                        pltpu.VMEM((NSA_GROUP, tk, tq), F32),
                        pltpu.VMEM((2 * NSA_GROUP, tq), F32),
                        pltpu.VMEM((2, NSA_GROUP, VT_ROWS, tq), F32)],
        compiler_params=_cparams(("parallel", "parallel", "arbitrary")),
        name="nsa",
    )(slopes, pbf, pbf, vt3, kcmp, kcmpt, pf, ovt, grp)


def _swa_kernel(slopes_ref, sinks_ref, q_ref, kv_ref, o_ref, *, tq):
    i = pl.program_id(1)
    q0 = i * tq
    kwid = tq + SWA_WINDOW
    start = pl.multiple_of(jnp.maximum(q0 - SWA_WINDOW, 0), SWA_WINDOW)
    kvw = kv_ref[pl.ds(start, kwid), :]
    dist = ((q0 - start) + lax.broadcasted_iota(jnp.int32, (tq, kwid), 0)
            - lax.broadcasted_iota(jnp.int32, (tq, kwid), 1)).astype(F32)
    valid = (dist >= 0.0) & (dist < float(SWA_WINDOW))
    lo_half = lax.broadcasted_iota(jnp.int32, (tq, LANES), 1) < HEAD_DIM
    qks = [_dot_nt(q_ref[:, h * LANES:(h + 1) * LANES], kvw) for h in range(SWA_HEADS)]
    es, dens = [], []
    for h in range(SWA_HEADS):
        s = jnp.where(valid, qks[h] - slopes_ref[h] * dist, NEG)
        sink = sinks_ref[h]
        m = jnp.maximum(jnp.max(s, axis=1, keepdims=True), sink)
        e = jnp.where(valid, jnp.exp(s - m), 0.0)
        dens.append(jnp.sum(e, axis=1, keepdims=True) + jnp.exp(sink - m))
        es.append(e.astype(BF16))
    outs = [_dot(es[h], kvw) / dens[h] for h in range(SWA_HEADS)]
    for pr in range(SWA_HEADS // 2):
        a_lo = pltpu.roll(outs[2 * pr], HEAD_DIM, 1)
        o_ref[:, pr * LANES:(pr + 1) * LANES] = jnp.where(lo_half, a_lo, outs[2 * pr + 1]).astype(BF16)


def _swa(slopes, sinks, pbf, batch, seq, tq, q_col, kv_col):
    nq = seq // tq
    qw = SWA_HEADS * LANES
    return pl.pallas_call(
        functools.partial(_swa_kernel, tq=tq),
        grid=(batch, nq),
        in_specs=[pl.BlockSpec(memory_space=pltpu.SMEM),
                  pl.BlockSpec(memory_space=pltpu.SMEM),
                  pl.BlockSpec((tq, qw), lambda b, i: (b * nq + i, q_col // qw)),
                  pl.BlockSpec((seq, LANES), lambda b, i: (b, kv_col // LANES))],
        out_specs=pl.BlockSpec((tq, SWA_HEADS * HEAD_DIM), lambda b, i: (b * nq + i, 0)),
        out_shape=jax.ShapeDtypeStruct((batch * seq, SWA_HEADS * HEAD_DIM), BF16),
        compiler_params=_cparams(("parallel", "parallel")),
        name="swa",
    )(slopes, sinks, pbf, pbf)


def _mid_kernel(x_ref, on_ref, os_ref, wo_ref, g2_ref, wq_ref, keys_ref,
                h_ref, h2_ref, st_ref):
    half = on_ref.shape[1]
    h = x_ref[...] + _dot(on_ref[...], wo_ref[:half, :]) + _dot(os_ref[...], wo_ref[half:, :])
    h_ref[...] = h
    ms = jnp.mean(h * h, axis=-1, keepdims=True)
    h2 = ((h * lax.rsqrt(ms + EPS)) * g2_ref[...]).astype(BF16)
    h2_ref[...] = h2
    qhs = [_dot(h2, wq_ref[:, 2 * hp * LANES:2 * (hp + 1) * LANES]).astype(BF16)
           for hp in range(PEER_HEADS)]
    for c in range(2 * PEER_HEADS):
        st_ref[c] = _dot_nt(keys_ref[c], qhs[c // 2][:, (c % 2) * LANES:(c % 2 + 1) * LANES])


def _mid(x2, o_n, o_s, wo, g2, wq, keys, tm):
    bt, d = x2.shape
    nk = 2 * PEER_HEADS
    return pl.pallas_call(
        _mid_kernel,
        grid=(bt // tm,),
        in_specs=[pl.BlockSpec((tm, d), lambda i: (i, 0)),
                  pl.BlockSpec((tm, o_n.shape[1]), lambda i: (i, 0)),
                  pl.BlockSpec((tm, o_s.shape[1]), lambda i: (i, 0)),
                  pl.BlockSpec(wo.shape, lambda i: (0, 0)),
                  pl.BlockSpec((1, d), lambda i: (0, 0)),
                  pl.BlockSpec(wq.shape, lambda i: (0, 0)),
                  pl.BlockSpec(keys.shape, lambda i: (0, 0, 0))],
        out_specs=[pl.BlockSpec((tm, d), lambda i: (i, 0)),
                   pl.BlockSpec((tm, d), lambda i: (i, 0)),
                   pl.BlockSpec((nk, PEER_NKEYS, tm), lambda i: (0, 0, i))],
        out_shape=[jax.ShapeDtypeStruct((bt, d), F32),
                   jax.ShapeDtypeStruct((bt, d), BF16),
                   jax.ShapeDtypeStruct((nk, PEER_NKEYS, bt), F32)],
        compiler_params=_cparams(("parallel",)),
        name="mid",
    )(x2, o_n, o_s, wo, g2, wq, keys)


def _pair_groups(k):
    groups = []
    i = 0
    while k // (i + 1) > 1:
        n = k // (i + 1)
        groups.append((i, n, -(-n // SUBLANES) * SUBLANES))
        i += 1
    return groups, i


def _peer_route_kernel(st_ref, w_ref, v_scr, i_scr, cand_scr, c_scr, f_scr, sa_scr, sb_scr, sg_scr,
                       at_scr, bt_scr, gt_scr, s_scr, *, tt, pitch, group):
    k = PEER_TOPK
    nk = PEER_NKEYS

    @pl.when(pl.program_id(0) == 0)
    def _():
        at_scr[...] = jnp.zeros(at_scr.shape, F32)
        bt_scr[...] = jnp.zeros(bt_scr.shape, F32)
        gt_scr[...] = jnp.zeros(gt_scr.shape, F32)

    groups, tail = _pair_groups(k)
    rid128 = lax.broadcasted_iota(jnp.int32, (nk, tt), 0).astype(F32)
    codes, pads = [], []
    for (i, n, rows) in groups:
        j = lax.broadcasted_iota(jnp.int32, (rows, tt), 0)
        codes.append((j + i * k).astype(F32))
        pads.append(j < n)
    jt = lax.broadcasted_iota(jnp.int32, (k - tail, tt), 0)
    codes.append(((jt + tail) * k).astype(F32))
    code = jnp.concatenate(codes, axis=0)
    sub = lax.broadcasted_iota(jnp.int32, (nk, at_scr.shape[1]), 0).astype(F32)

    def build_group(tg):
        xs, ys = [], []
        for u in range(group):
            t = tg * group + u
            xs.append(jnp.where(at_scr[t:t + 1, :] == sub, gt_scr[t:t + 1, :], 0.0).astype(BF16))
            ys.append(jnp.where(bt_scr[t:t + 1, :] == sub, 1.0, 0.0).astype(BF16))
        ws = [_dot_nt(xs[u], ys[u]) for u in range(group)]
        for u in range(group):
            s_scr[pl.ds(tg * group + u, nk, stride=pitch), :] = ws[u]

    def convert_blocks(lo, hi):
        for i1 in range(lo, hi):
            w_ref[i1] = s_scr[i1 * pitch:i1 * pitch + tt, :].astype(BF16)

    def topk_head(h):
        for c in range(2):
            for r, (m, idx) in enumerate(_topk_axis0(st_ref[2 * h + c], k, rid128)):
                v_scr[c, r:r + 1, :] = m
                i_scr[c, r:r + 1, :] = idx
        row = 0
        for gi, (i, n, rows) in enumerate(groups):
            vals = v_scr[0, i:i + 1, :] + v_scr[1, 0:rows, :]
            cand_scr[row:row + rows, :] = jnp.where(pads[gi], vals, NEG)
            row += rows
        cand_scr[row:row + k - tail, :] = v_scr[0, tail:k, :] + v_scr[1, 0:1, :]
        for r, (m, f) in enumerate(_topk_axis0(cand_scr[...], k, code)):
            c_scr[r:r + 1, :] = m
            f_scr[r:r + 1, :] = f
        cs, fl = c_scr[...], f_scr[...]
        fi = jnp.floor(fl * (1.0 / k))
        fj = fl - fi * k
        a = jnp.zeros_like(fl)
        b = jnp.zeros_like(fl)
        for r in range(k):
            a = jnp.where(fi == float(r), i_scr[0, r:r + 1, :], a)
            b = jnp.where(fj == float(r), i_scr[1, r:r + 1, :], b)
        e = jnp.exp(cs - cs[0:1, :])
        sa_scr[h * k:(h + 1) * k, :] = a
        sb_scr[h * k:(h + 1) * k, :] = b
        sg_scr[h * k:(h + 1) * k, :] = e / jnp.sum(e, axis=0, keepdims=True)
    n_groups = tt // group
    early = PEER_HEADS // 2
    h = 0
    for tg in range(n_groups):
        build_group(tg)
        if h < early and (tg + 1) * early >= (h + 1) * n_groups:
            topk_head(h)
            h += 1
    while h < early:
        topk_head(h)
        h += 1
    blk = nk // (PEER_HEADS - early)
    for q in range(PEER_HEADS - early):
        convert_blocks(q * blk, (q + 1) * blk)
        topk_head(early + q)
    at_scr[...] = sa_scr[...].T
    bt_scr[...] = sb_scr[...].T
    gt_scr[...] = sg_scr[...].T


def _peer_route(st, tt, group):
    nkk, _, bt = st.shape
    k = PEER_TOPK
    nk = PEER_NKEYS
    ns = PEER_HEADS * k
    nt = bt // tt
    groups, tail = _pair_groups(k)
    n_cand = sum(rows for _, _, rows in groups) + k - tail
    pitch = tt + SUBLANES
    return pl.pallas_call(
        functools.partial(_peer_route_kernel, tt=tt, pitch=pitch, group=group),
        grid=(nt + 1,),
        in_specs=[pl.BlockSpec((nkk, nk, tt), lambda i: (0, 0, jnp.minimum(i, nt - 1)))],
        out_specs=pl.BlockSpec((nk, tt, nk), lambda i: (0, jnp.maximum(i - 1, 0), 0)),
        out_shape=jax.ShapeDtypeStruct((nk, bt, nk), BF16),
        scratch_shapes=[pltpu.VMEM((2, k, tt), F32), pltpu.VMEM((2, k, tt), F32),
                        pltpu.VMEM((n_cand, tt), F32),
                        pltpu.VMEM((k, tt), F32), pltpu.VMEM((k, tt), F32),
                        pltpu.VMEM((ns, tt), F32), pltpu.VMEM((ns, tt), F32), pltpu.VMEM((ns, tt), F32),
                        pltpu.VMEM((tt, ns), F32), pltpu.VMEM((tt, ns), F32), pltpu.VMEM((tt, ns), F32),
                        pltpu.VMEM((nk * pitch, nk), F32)],
        compiler_params=_cparams(("arbitrary",)),
        name="peer_route",
    )(st)


def _peer_ffn_kernel(h2_ref, ue_ref, uo_ref, ve_ref, vo_ref, we_ref, wo_ref, h_ref, gf_ref, o_ref,
                     acc_scr, act0, act1):
    j = pl.program_id(1)
    last = pl.num_programs(1) - 1

    @pl.when(j == 0)
    def _():
        acc_scr[...] = jnp.zeros(acc_scr.shape, F32)
        act1[...] = jnp.zeros(act1.shape, F32)

    def consume(act_ref, w_ref, v_ref):
        z = [(jax.nn.gelu(act_ref[:, c * LANES:(c + 1) * LANES]) * w_ref[c].astype(F32)).astype(BF16)
             for c in range(w_ref.shape[0])]
        acc_scr[...] += _dot(jnp.concatenate(z, axis=1), v_ref[...])

    h2 = h2_ref[...]
    act0[...] = _dot_nt(h2, ue_ref[...])
    consume(act1, we_ref, ve_ref)

    @pl.when(j < last)
    def _():
        act1[...] = _dot_nt(h2, uo_ref[...])
        consume(act0, wo_ref, vo_ref)

    @pl.when(j == last)
    def _():
        h = h_ref[...] + acc_scr[...]
        ms = jnp.mean(h * h, axis=-1, keepdims=True)
        o_ref[...] = (h * lax.rsqrt(ms + EPS)) * gf_ref[...]


def _peer_ffn(h2, u, v, w, h, gf, tm, te):
    bt, d = h.shape
    ne = u.shape[0] // te
    assert ne % 2 == 0
    nb = te // LANES
    top = ne - 1
    produce_e = lambda i, j: (jnp.minimum(2 * j, top), 0)
    produce_o = lambda i, j: (jnp.minimum(2 * j + 1, top), 0)
    consume_e = lambda i, j: (jnp.maximum(2 * j - 1, 0), 0)
    consume_o = lambda i, j: (jnp.minimum(2 * j, top), 0)
    return pl.pallas_call(
        _peer_ffn_kernel,
        grid=(bt // tm, ne // 2 + 1),
        in_specs=[pl.BlockSpec((tm, d), lambda i, j: (i, 0)),
                  pl.BlockSpec((te, d), produce_e),
                  pl.BlockSpec((te, d), produce_o),
                  pl.BlockSpec((te, d), consume_e),
                  pl.BlockSpec((te, d), consume_o),
                  pl.BlockSpec((nb, tm, LANES), lambda i, j: (jnp.maximum(2 * j - 1, 0), i, 0)),
                  pl.BlockSpec((nb, tm, LANES), lambda i, j: (jnp.minimum(2 * j, top), i, 0)),
                  pl.BlockSpec((tm, d), lambda i, j: (i, 0)),
                  pl.BlockSpec((1, d), lambda i, j: (0, 0))],
        out_specs=pl.BlockSpec((tm, d), lambda i, j: (i, 0)),
        out_shape=jax.ShapeDtypeStruct((bt, d), F32),
        scratch_shapes=[pltpu.VMEM((tm, d), F32), pltpu.VMEM((tm, te), F32), pltpu.VMEM((tm, te), F32)],
        compiler_params=_cparams(("parallel", "arbitrary")),
        name="peer_ffn",
    )(h2, u, u, v, v, w, w, h, gf)


def _split_offsets():
    hd = HEAD_DIM
    sizes = [NSA_HEADS * hd, NSA_KV * hd, NSA_KV * hd, NSA_KV * hd, NSA_KV * hd, NSA_KV * hd,
             NSA_KV * hd, NSA_HEADS * 3, SWA_HEADS * hd, hd, hd]
    return [0] + [int(c) for c in np.cumsum(sizes)]


def _pick(n, prefs):
    for p in prefs:
        if n % p == 0:
            return p
    return n


def kernel(x, ln1_g, w_in, cmp_pe_k, cmp_w1_k, cmp_w2_k, cmp_pe_v, cmp_w1_v, cmp_w2_v, swa_sinks, w_out, ln2_g, peer_wq, peer_keys, peer_u, peer_v, lnf_g):
    batch, seq, d = x.shape
    bt = batch * seq
    hd = HEAD_DIM
    nsa_tq, nsa_tk = 256, 256
    assert ln1_g.shape[0] == 1, "single layer"
    assert seq % 512 == 0
    slopes = jnp.asarray((2.0 ** (-8.0 * (np.arange(N_HEADS) + 1) / N_HEADS)).astype(np.float32))

    off = _split_offsets()
    w = w_in[0]
    col = lambda k: w[:, off[k]:off[k + 1]]
    scale = hd ** -0.5
    z64 = jnp.zeros((d, hd), F32)
    qn, qs = col(0) * (scale * LOG2E), col(8) * scale
    qn_exp = []
    for h in range(NSA_HEADS):
        qh = qn[:, h * hd:(h + 1) * hd]
        qn_exp += [qh, z64] if h // NSA_GROUP == 0 else [z64, qh]
    qs_exp = []
    for h in range(SWA_HEADS):
        qs_exp += [qs[:, h * hd:(h + 1) * hd], z64]
    gt = col(7)
    gpad = jnp.zeros((d, LANES - NSA_GROUP * 3), F32)
    gcols = []
    for g in range(NSA_KV):
        gcols += [gt[:, g * NSA_GROUP * 3:(g + 1) * NSA_GROUP * 3], gpad]
    w_all = jnp.concatenate(qn_exp + qs_exp + [col(3), col(5), col(9), col(10)]
                            + [col(4), col(6)] + [col(1), col(2)] + gcols, axis=1).astype(BF16)
    swa_q_col = NSA_HEADS * LANES
    nsa_k_col = swa_q_col + SWA_HEADS * LANES
    swa_kv_col = nsa_k_col + 2 * LANES
    n_bf = swa_kv_col + LANES
    n_vt = 2 * LANES

    x2 = x.reshape(bt, d)
    pbf, vt3, pf = _proj(x2, ln1_g[0][None, :], w_all, n_bf, n_vt, 512, nsa_tk)

    ncp = seq // NSA_CMP_STRIDE
    c_all = pf[:, :2 * LANES].reshape(batch, ncp, NSA_CMP_STRIDE, 2, NSA_KV, hd)
    c_all = c_all.transpose(3, 0, 4, 1, 2, 5).reshape(2, batch, NSA_KV, ncp, NSA_CMP_STRIDE * hd)
    pe_all = jnp.stack([cmp_pe_k[0].reshape(2, -1), cmp_pe_v[0].reshape(2, -1)])
    w1_all = jnp.stack([cmp_w1_k[0], cmp_w1_v[0]]).astype(BF16)
    zc = jnp.zeros((NSA_CMP_HIDDEN, hd), F32)
    w2e = jnp.stack([jnp.stack([jnp.concatenate([w2, zc], axis=1), jnp.concatenate([zc, w2], axis=1)])
                     for w2 in (cmp_w2_k[0], cmp_w2_v[0])]).astype(BF16)
    kcmp, kcmpt = _compress(c_all, pe_all, w1_all, w2e)

    n_sel = seq // NSA_SEL_BLOCK
    c0 = np.arange(ncp)[None, :] * NSA_CMP_STRIDE
    s0 = np.arange(n_sel)[:, None] * NSA_SEL_BLOCK
    ovt = ((c0 < s0 + NSA_SEL_BLOCK) & (c0 + NSA_CMP_LEN > s0) & (np.arange(ncp)[None, :] < ncp - 1))
    ovt = jnp.asarray(ovt.astype(np.float32), BF16)
    grp = (np.arange(n_sel)[None, :] * NSA_SEL_BLOCK // nsa_tk) == np.arange(seq // nsa_tk)[:, None]
    grp = jnp.asarray(grp.astype(np.float32), BF16)

    o_n = _nsa(slopes, pbf, vt3, pf, kcmp, kcmpt, ovt, grp, batch, seq, nsa_tq, nsa_tk, nsa_k_col)
    o_s = _swa(slopes, swa_sinks[0], pbf, batch, seq, 128, swa_q_col, swa_kv_col)

    keys = peer_keys[0].reshape(2 * PEER_HEADS, PEER_NKEYS, -1).astype(BF16)
    h, h2, st = _mid(x2, o_n, o_s, w_out[0].astype(BF16), ln2_g[0][None, :],
                     peer_wq[0].astype(BF16), keys, _pick(bt, (256, 128)))
    wd = _peer_route(st, 128, 32)
    out = _peer_ffn(h2, peer_u[0].astype(BF16), peer_v[0].astype(BF16), wd, h, lnf_g[None, :],
                    _pick(bt, (1024, 512, 256, 128)), 512)
    return out.reshape(batch, seq, d)
```

```python
import functools

import numpy as np
import jax
import jax.numpy as jnp
from jax import lax
from jax.experimental import pallas as pl
from jax.experimental.pallas import tpu as pltpu

F32 = jnp.float32
BF16 = jnp.bfloat16

HEAD_DIM = 64
N_HEADS = 16
NSA_HEADS = 8
NSA_KV = 2
NSA_GROUP = 4
SWA_HEADS = 8
NSA_CMP_LEN = 32
NSA_CMP_STRIDE = 16
NSA_CMP_HIDDEN = 256
NSA_SEL_BLOCK = 64
NSA_TOPN = 16
NSA_WINDOW = 512
SWA_WINDOW = 128
FORCE_SCORE = 1.0e4
PEER_HEADS = 8
PEER_NKEYS = 128
PEER_TOPK = 16
EPS = 1e-6
NEG = -1.0e30
LANES = 128
SUBLANES = 8
BF16_ROWS = 16
VT_ROWS = HEAD_DIM + BF16_ROWS
LOG2E = 1.4426950408889634
VMEM_LIMIT = 56 * 1024 * 1024


def _dot(a, b):
    return jnp.dot(a, b, preferred_element_type=F32)


def _dot_nt(a, b):
    return lax.dot_general(a, b, (((1,), (1,)), ((), ())), preferred_element_type=F32)


def _cparams(sem):
    return pltpu.CompilerParams(dimension_semantics=sem, vmem_limit_bytes=VMEM_LIMIT)


def _topk_axis0(v, k, code):
    big = 3.0e38
    out = []
    for _ in range(k):
        m = jnp.max(v, axis=0, keepdims=True)
        c = jnp.min(jnp.where(v == m, code, big), axis=0, keepdims=True)
        out.append((m, c))
        v = jnp.where(code == c, NEG, v)
    return out


def _proj_kernel(x_ref, g_ref, w_ref, obf_ref, vt_ref, of_ref, *, n_bf, n_vt, tk):
    x = x_ref[...]
    ms = jnp.mean(x * x, axis=-1, keepdims=True)
    a = ((x * lax.rsqrt(ms + EPS)) * g_ref[...]).astype(BF16)
    for c0 in range(0, n_bf, 4 * LANES):
        c1 = min(c0 + 4 * LANES, n_bf)
        obf_ref[:, c0:c1] = _dot(a, w_ref[:, c0:c1]).astype(BF16)
    yvt = _dot(a, w_ref[:, n_bf:n_bf + n_vt]).T
    ones = jnp.ones((VT_ROWS - HEAD_DIM, tk), BF16)
    for s in range(vt_ref.shape[0]):
        for c in range(n_vt // HEAD_DIM):
            vt_ref[s, c * VT_ROWS:c * VT_ROWS + HEAD_DIM, :] = (
                yvt[c * HEAD_DIM:(c + 1) * HEAD_DIM, s * tk:(s + 1) * tk].astype(BF16))
            vt_ref[s, c * VT_ROWS + HEAD_DIM:(c + 1) * VT_ROWS, :] = ones
    of_ref[...] = _dot(a, w_ref[:, n_bf + n_vt:])


def _proj(x2, g, w_all, n_bf, n_vt, tm, tk):
    bt, d = x2.shape
    n_all = w_all.shape[1]
    n_f = n_all - n_bf - n_vt
    vt_rows = n_vt // HEAD_DIM * VT_ROWS
    return pl.pallas_call(
        functools.partial(_proj_kernel, n_bf=n_bf, n_vt=n_vt, tk=tk),
        grid=(bt // tm,),
        in_specs=[pl.BlockSpec((tm, d), lambda i: (i, 0)),
                  pl.BlockSpec((1, d), lambda i: (0, 0)),
                  pl.BlockSpec((d, n_all), lambda i: (0, 0))],
        out_specs=[pl.BlockSpec((tm, n_bf), lambda i: (i, 0)),
                   pl.BlockSpec((tm // tk, vt_rows, tk), lambda i: (i, 0, 0)),
                   pl.BlockSpec((tm, n_f), lambda i: (i, 0))],
        out_shape=[jax.ShapeDtypeStruct((bt, n_bf), BF16),
                   jax.ShapeDtypeStruct((bt // tk, vt_rows, tk), BF16),
                   jax.ShapeDtypeStruct((bt, n_f), F32)],
        compiler_params=_cparams(("parallel",)),
        name="proj",
    )(x2, g, w_all)


def _compress_kernel(c_ref, pe_ref, w1_ref, w2_ref, o_ref, ot_ref, *, ncp):
    lo_half = lax.broadcasted_iota(jnp.int32, (ncp, LANES), 1) < HEAD_DIM
    xa = [[], []]
    xb = [[], []]
    for l in range(NSA_CMP_STRIDE):
        x = c_ref[pl.ds(l, ncp, stride=NSA_CMP_STRIDE), :]
        a = x + pe_ref[l:l + 1, :]
        b = x + pe_ref[NSA_CMP_STRIDE + l:NSA_CMP_STRIDE + l + 1, :]
        xa[0].append(jnp.where(lo_half, a, 0.0).astype(BF16))
        xa[1].append(jnp.where(lo_half, 0.0, a).astype(BF16))
        xb[0].append(jnp.where(lo_half, b, 0.0).astype(BF16))
        xb[1].append(jnp.where(lo_half, 0.0, b).astype(BF16))
    acc = None
    for g in range(NSA_KV):
        ya = _dot(jnp.concatenate(xa[g], axis=1), w1_ref[0])
        yb = _dot(jnp.concatenate(xb[g], axis=1), w1_ref[1])
        hid = ya + pltpu.roll(yb, ncp - 1, 0)
        act = jax.nn.gelu(hid).astype(BF16)
        t = _dot(act, w2_ref[g])
        acc = t if acc is None else acc + t
    o_ref[...] = acc.astype(BF16)
    ot_ref[...] = acc.T.astype(BF16)


def _compress(pf, pe_dup, w1_dup, w2e, batch, seq):
    ncp = seq // NSA_CMP_STRIDE
    kdim = NSA_CMP_STRIDE * LANES
    return pl.pallas_call(
        functools.partial(_compress_kernel, ncp=ncp),
        grid=(2, batch),
        in_specs=[pl.BlockSpec((seq, LANES), lambda s, i: (i, s)),
                  pl.BlockSpec((None, NSA_CMP_LEN, LANES), lambda s, i: (s, 0, 0)),
                  pl.BlockSpec((None, 2, kdim, NSA_CMP_HIDDEN), lambda s, i: (s, 0, 0, 0)),
                  pl.BlockSpec((None, NSA_KV, NSA_CMP_HIDDEN, LANES), lambda s, i: (s, 0, 0, 0))],
        out_specs=[pl.BlockSpec((None, None, ncp, LANES), lambda s, i: (s, i, 0, 0)),
                   pl.BlockSpec((None, None, LANES, ncp), lambda s, i: (s, i, 0, 0))],
        out_shape=[jax.ShapeDtypeStruct((2, batch, ncp, LANES), BF16),
                   jax.ShapeDtypeStruct((2, batch, LANES, ncp), BF16)],
        compiler_params=_cparams(("parallel", "parallel")),
        name="compress",
    )(pf, pe_dup, w1_dup, w2e)


def _nsa_kernel(slopes_ref, q_ref, k_ref, vt_ref, kc_ref, vct_ref, gate_ref, ovt_ref, grp_ref, o_ref,
                qt_scr, oc_scr, bias_scr, m_scr, acc_scr, *, tq, tk, seq, ncp):
    i = pl.program_id(1)
    g = pl.program_id(2)
    q0 = i * tq
    n_sel = seq // NSA_SEL_BLOCK
    n_cmp = ncp - 1
    slope = [slopes_ref[SWA_HEADS + NSA_GROUP * g + hh] * LOG2E for hh in range(NSA_GROUP)]

    kc = kc_ref[...]
    vct = vct_ref[...]
    n_c = lax.broadcasted_iota(jnp.int32, (ncp, tq), 0)
    t_c = q0 + lax.broadcasted_iota(jnp.int32, (ncp, tq), 1)
    dist_c = (t_c - (n_c * NSA_CMP_STRIDE + (NSA_CMP_LEN - 1))).astype(F32)
    valid_c = (dist_c >= 0.0) & (n_c < n_cmp)
    psum = jnp.zeros((ncp, tq), F32)
    qts = []
    for hh in range(NSA_GROUP):
        qts.append(q_ref[:, hh * LANES:(hh + 1) * LANES].astype(F32).T.astype(BF16))
        qt_scr[hh] = qts[hh]
    scs = [_dot(kc, qt) for qt in qts]
    pcs = []
    for hh in range(NSA_GROUP):
        s = jnp.where(valid_c, scs[hh] - slope[hh] * dist_c, NEG)
        m = jnp.max(s, axis=0, keepdims=True)
        e = jnp.where(valid_c, jnp.exp2(s - m), 0.0)
        den = jnp.maximum(jnp.sum(e, axis=0, keepdims=True), 1e-30)
        p = e * (1.0 / den)
        psum = psum + p
        pcs.append(p.astype(BF16))
    for hh in range(NSA_GROUP):
        oc_scr[hh] = _dot(vct, pcs[hh])

    imp = _dot(ovt_ref[...], psum.astype(BF16))
    jb = lax.broadcasted_iota(jnp.int32, (n_sel, tq), 0)
    blk_t = (q0 + lax.broadcasted_iota(jnp.int32, (n_sel, tq), 1)) // NSA_SEL_BLOCK
    forced = ((jb == 0) | (jb == blk_t) | (jb == blk_t - 1)) & (jb <= blk_t)
    imp = jnp.where(forced | (jb > blk_t), NEG, imp)
    jbf = jb.astype(F32)
    sel_t = jnp.where(forced, 1.0, 0.0)
    for m_r, idx_r in _topk_axis0(imp, min(NSA_TOPN, n_sel) - 3, jbf):
        sel_t = jnp.where((jbf == idx_r) & (m_r > 0.5 * NEG), 1.0, sel_t)
    sel = sel_t.astype(BF16)
    nkt = grp_ref.shape[0]
    cnt = jnp.max(_dot(grp_ref[...], sel), axis=1, keepdims=True)
    tid = lax.broadcasted_iota(jnp.int32, (nkt, 1), 0)
    j_sel = jnp.min(jnp.where((cnt > 0.5) & (tid > 0), tid.astype(F32), float(nkt))).astype(jnp.int32)
    only_block0 = jnp.max(cnt[0:1, :]) < 1.5

    m_scr[...] = jnp.full(m_scr.shape, NEG, F32)
    acc_scr[...] = jnp.zeros(acc_scr.shape, F32)
    rel = (lax.broadcasted_iota(jnp.int32, (tk, tq), 1)
           - lax.broadcasted_iota(jnp.int32, (tk, tq), 0)).astype(F32)
    for hh in range(NSA_GROUP):
        bias_scr[hh] = slope[hh] * rel
    e_rel = (lax.broadcasted_iota(jnp.int32, (tk, n_sel), 1)
             - lax.broadcasted_iota(jnp.int32, (tk, n_sel), 0) // NSA_SEL_BLOCK)

    def flash_update(branches, off):
        qks = [[_dot(k_tile, qt_scr[hh]) for hh in range(NSA_GROUP)] for _, k_tile, _, _ in branches]
        ps, alphas = [], []
        for bi, (br, k_tile, valid, _) in enumerate(branches):
            for hh in range(NSA_GROUP):
                r = br * NSA_GROUP + hh
                shift = slope[hh] * off
                s = qks[bi][hh] - bias_scr[hh, 0:k_tile.shape[0], :]
                if valid is not None:
                    s = jnp.where(valid, s, NEG)
                m_old = m_scr[r:r + 1, :]
                m_new = jnp.maximum(m_old, jnp.max(s, axis=0, keepdims=True) - shift)
                alphas.append(jnp.exp2(m_old - m_new))
                ps.append(jnp.exp2(s - (m_new + shift)).astype(BF16))
                m_scr[r:r + 1, :] = m_new
        for bi, (br, _, _, vt) in enumerate(branches):
            for hh in range(NSA_GROUP):
                n = bi * NSA_GROUP + hh
                acc_scr[br, hh] = alphas[n] * acc_scr[br, hh] + _dot(vt, ps[n])

    vrow = pl.multiple_of(g * VT_ROWS, BF16_ROWS)

    def tile(j, with_window):
        k0 = j * tk if isinstance(j, int) else pl.multiple_of(j * tk, tk)
        off = (q0 - k0).astype(F32)
        dist = rel + off
        causal = dist >= 0.0
        expand = (e_rel == k0 // NSA_SEL_BLOCK).astype(BF16)
        branches = [(0, k_ref[pl.ds(k0, tk), 0:LANES], (_dot(expand, sel) > 0.5) & causal,
                     vt_ref[j, pl.ds(vrow, VT_ROWS), :])]
        if with_window:
            branches.append((1, k_ref[pl.ds(k0, tk), LANES:2 * LANES], causal & (dist < float(NSA_WINDOW)),
                             vt_ref[j, pl.ds(NSA_KV * VT_ROWS + vrow, VT_ROWS), :]))
        flash_update(branches, off)

    j_win = jnp.maximum(q0 - (NSA_WINDOW - 1), 0) // tk
    j_lo = jnp.minimum(j_sel, j_win)

    @pl.when((j_lo > 0) & jnp.logical_not(only_block0))
    def _():
        tile(0, False)

    @pl.when((j_lo > 0) & only_block0)
    def _():
        nb = NSA_SEL_BLOCK
        flash_update([(0, k_ref[0:nb, 0:LANES], None, vt_ref[0, pl.ds(vrow, VT_ROWS), 0:nb])], q0.astype(F32))

    def body(j, carry):
        @pl.when(j < j_win)
        def _():
            tile(j, False)

        @pl.when(j >= j_win)
        def _():
            tile(j, True)

        return carry

    lax.fori_loop(j_lo, (q0 + tq + tk - 1) // tk, body, 0)

    gst = jax.nn.sigmoid(gate_ref[...]).T
    r0 = pl.multiple_of(g * HEAD_DIM, HEAD_DIM)
    heads = []
    for hh in range(NSA_GROUP):
        inv_s = 1.0 / acc_scr[0, hh, HEAD_DIM:HEAD_DIM + 1, :]
        inv_w = 1.0 / acc_scr[1, hh, HEAD_DIM:HEAD_DIM + 1, :]
        heads.append(gst[3 * hh:3 * hh + 1, :] * oc_scr[hh, pl.ds(r0, HEAD_DIM), :]
                     + (gst[3 * hh + 1:3 * hh + 2, :] * inv_s) * acc_scr[0, hh, 0:HEAD_DIM, :]
                     + (gst[3 * hh + 2:3 * hh + 3, :] * inv_w) * acc_scr[1, hh, 0:HEAD_DIM, :])
    o_ref[...] = jnp.concatenate(heads, axis=0).T.astype(BF16)


def _nsa(slopes, pbf, vt3, pf, kcmp, kcmpt, ovt, grp, batch, seq, tq, tk, k_col):
    nq = seq // tq
    ncp = kcmp.shape[2]
    n_sel = seq // NSA_SEL_BLOCK
    nkt = seq // tk
    gw = NSA_GROUP * LANES
    return pl.pallas_call(
        functools.partial(_nsa_kernel, tq=tq, tk=tk, seq=seq, ncp=ncp),
        grid=(batch, nq, NSA_KV),
        in_specs=[pl.BlockSpec(memory_space=pltpu.SMEM),
                  pl.BlockSpec((tq, gw), lambda b, i, g: (b * nq + i, g)),
                  pl.BlockSpec((seq, 2 * LANES), lambda b, i, g: (b, k_col // (2 * LANES))),
                  pl.BlockSpec((nkt, vt3.shape[1], tk), lambda b, i, g: (b, 0, 0)),
                  pl.BlockSpec((None, None, ncp, LANES), lambda b, i, g: (0, b, 0, 0)),
                  pl.BlockSpec((None, None, LANES, ncp), lambda b, i, g: (1, b, 0, 0)),
                  pl.BlockSpec((tq, LANES), lambda b, i, g: (b * nq + i, 2 + g)),
                  pl.BlockSpec((n_sel, ncp), lambda b, i, g: (0, 0)),
                  pl.BlockSpec((nkt, n_sel), lambda b, i, g: (0, 0))],
        out_specs=pl.BlockSpec((tq, NSA_GROUP * HEAD_DIM), lambda b, i, g: (b * nq + i, g)),
        out_shape=jax.ShapeDtypeStruct((batch * seq, NSA_HEADS * HEAD_DIM), BF16),
        scratch_shapes=[pltpu.VMEM((NSA_GROUP, LANES, tq), BF16),
                        pltpu.VMEM((NSA_GROUP, LANES, tq), F32),
                        pltpu.VMEM((NSA_GROUP, tk, tq), F32),
                        pltpu.VMEM((2 * NSA_GROUP, tq), F32),
                        pltpu.VMEM((2, NSA_GROUP, VT_ROWS, tq), F32)],
        compiler_params=_cparams(("parallel", "parallel", "arbitrary")),
        name="nsa",
    )(slopes, pbf, pbf, vt3, kcmp, kcmpt, pf, ovt, grp)


def _swa_kernel(slopes_ref, sinks_ref, q_ref, kv_ref, o_ref, *, tq):
    i = pl.program_id(1)
    q0 = i * tq
    kwid = tq + SWA_WINDOW
    start = pl.multiple_of(jnp.maximum(q0 - SWA_WINDOW, 0), SWA_WINDOW)
    kvw = kv_ref[pl.ds(start, kwid), :]
    dist = ((q0 - start) + lax.broadcasted_iota(jnp.int32, (tq, kwid), 0)
            - lax.broadcasted_iota(jnp.int32, (tq, kwid), 1)).astype(F32)
    valid = (dist >= 0.0) & (dist < float(SWA_WINDOW))
    lo_half = lax.broadcasted_iota(jnp.int32, (tq, LANES), 1) < HEAD_DIM
    qks = [_dot_nt(q_ref[:, h * LANES:(h + 1) * LANES], kvw) for h in range(SWA_HEADS)]
    es, dens = [], []
    for h in range(SWA_HEADS):
        s = jnp.where(valid, qks[h] - slopes_ref[h] * dist, NEG)
        sink = sinks_ref[h]
        m = jnp.maximum(jnp.max(s, axis=1, keepdims=True), sink)
        e = jnp.where(valid, jnp.exp(s - m), 0.0)
        dens.append(jnp.sum(e, axis=1, keepdims=True) + jnp.exp(sink - m))
        es.append(e.astype(BF16))
    outs = [_dot(es[h], kvw) / dens[h] for h in range(SWA_HEADS)]
    for pr in range(SWA_HEADS // 2):
        a_lo = pltpu.roll(outs[2 * pr], HEAD_DIM, 1)
        o_ref[:, pr * LANES:(pr + 1) * LANES] = jnp.where(lo_half, a_lo, outs[2 * pr + 1]).astype(BF16)


def _swa(slopes, sinks, pbf, batch, seq, tq, q_col, kv_col):
    nq = seq // tq
    qw = SWA_HEADS * LANES
    return pl.pallas_call(
        functools.partial(_swa_kernel, tq=tq),
        grid=(batch, nq),
        in_specs=[pl.BlockSpec(memory_space=pltpu.SMEM),
                  pl.BlockSpec(memory_space=pltpu.SMEM),
                  pl.BlockSpec((tq, qw), lambda b, i: (b * nq + i, q_col // qw)),
                  pl.BlockSpec((seq, LANES), lambda b, i: (b, kv_col // LANES))],
        out_specs=pl.BlockSpec((tq, SWA_HEADS * HEAD_DIM), lambda b, i: (b * nq + i, 0)),
        out_shape=jax.ShapeDtypeStruct((batch * seq, SWA_HEADS * HEAD_DIM), BF16),
        compiler_params=_cparams(("parallel", "parallel")),
        name="swa",
    )(slopes, sinks, pbf, pbf)


def _mid_kernel(x_ref, on_ref, os_ref, wo_ref, g2_ref, wq_ref, keys_ref,
                h_ref, h2_ref, st_ref):
    half = on_ref.shape[1]
    h = x_ref[...] + _dot(on_ref[...], wo_ref[:half, :]) + _dot(os_ref[...], wo_ref[half:, :])
    h_ref[...] = h
    ms = jnp.mean(h * h, axis=-1, keepdims=True)
    h2 = ((h * lax.rsqrt(ms + EPS)) * g2_ref[...]).astype(BF16)
    h2_ref[...] = h2
    qhs = [_dot(h2, wq_ref[:, 2 * hp * LANES:2 * (hp + 1) * LANES]).astype(BF16)
           for hp in range(PEER_HEADS)]
    for c in range(2 * PEER_HEADS):
        st_ref[c] = _dot_nt(keys_ref[c], qhs[c // 2][:, (c % 2) * LANES:(c % 2 + 1) * LANES])


def _mid(x2, o_n, o_s, wo, g2, wq, keys, tm):
    bt, d = x2.shape
    nk = 2 * PEER_HEADS
    return pl.pallas_call(
        _mid_kernel,
        grid=(bt // tm,),
        in_specs=[pl.BlockSpec((tm, d), lambda i: (i, 0)),
                  pl.BlockSpec((tm, o_n.shape[1]), lambda i: (i, 0)),
                  pl.BlockSpec((tm, o_s.shape[1]), lambda i: (i, 0)),
                  pl.BlockSpec(wo.shape, lambda i: (0, 0)),
                  pl.BlockSpec((1, d), lambda i: (0, 0)),
                  pl.BlockSpec(wq.shape, lambda i: (0, 0)),
                  pl.BlockSpec(keys.shape, lambda i: (0, 0, 0))],
        out_specs=[pl.BlockSpec((tm, d), lambda i: (i, 0)),
                   pl.BlockSpec((tm, d), lambda i: (i, 0)),
                   pl.BlockSpec((nk, PEER_NKEYS, tm), lambda i: (0, 0, i))],
        out_shape=[jax.ShapeDtypeStruct((bt, d), F32),
                   jax.ShapeDtypeStruct((bt, d), BF16),
                   jax.ShapeDtypeStruct((nk, PEER_NKEYS, bt), F32)],
        compiler_params=_cparams(("parallel",)),
        name="mid",
    )(x2, o_n, o_s, wo, g2, wq, keys)


def _pair_groups(k):
    groups = []
    i = 0
    while k // (i + 1) > 1:
        n = k // (i + 1)
        groups.append((i, n, -(-n // SUBLANES) * SUBLANES))
        i += 1
    return groups, i


def _peer_route_kernel(st_ref, w_ref, v_scr, i_scr, cand_scr, c_scr, f_scr, sa_scr, sb_scr, sg_scr,
                       at_scr, bt_scr, gt_scr, s_scr, *, tt, pitch, group):
    k = PEER_TOPK
    nk = PEER_NKEYS

    @pl.when(pl.program_id(0) == 0)
    def _():
        at_scr[...] = jnp.zeros(at_scr.shape, F32)
        bt_scr[...] = jnp.zeros(bt_scr.shape, F32)
        gt_scr[...] = jnp.zeros(gt_scr.shape, F32)

    groups, tail = _pair_groups(k)
    rid128 = lax.broadcasted_iota(jnp.int32, (nk, tt), 0).astype(F32)
    codes, pads = [], []
    for (i, n, rows) in groups:
        j = lax.broadcasted_iota(jnp.int32, (rows, tt), 0)
        codes.append((j + i * k).astype(F32))
        pads.append(j < n)
    jt = lax.broadcasted_iota(jnp.int32, (k - tail, tt), 0)
    codes.append(((jt + tail) * k).astype(F32))
    code = jnp.concatenate(codes, axis=0)
    sub = lax.broadcasted_iota(jnp.int32, (nk, at_scr.shape[1]), 0).astype(F32)

    def build_group(tg):
        xs, ys = [], []
        for u in range(group):
            t = tg * group + u
            xs.append(jnp.where(at_scr[t:t + 1, :] == sub, gt_scr[t:t + 1, :], 0.0).astype(BF16))
            ys.append(jnp.where(bt_scr[t:t + 1, :] == sub, 1.0, 0.0).astype(BF16))
        ws = [_dot_nt(xs[u], ys[u]) for u in range(group)]
        for u in range(group):
            s_scr[pl.ds(tg * group + u, nk, stride=pitch), :] = ws[u]

    def convert_blocks(lo, hi):
        for i1 in range(lo, hi):
            w_ref[i1] = s_scr[i1 * pitch:i1 * pitch + tt, :].astype(BF16)

    def topk_head(h):
        for c in range(2):
            for r, (m, idx) in enumerate(_topk_axis0(st_ref[2 * h + c], k, rid128)):
                v_scr[c, r:r + 1, :] = m
                i_scr[c, r:r + 1, :] = idx
        row = 0
        for gi, (i, n, rows) in enumerate(groups):
            vals = v_scr[0, i:i + 1, :] + v_scr[1, 0:rows, :]
            cand_scr[row:row + rows, :] = jnp.where(pads[gi], vals, NEG)
            row += rows
        cand_scr[row:row + k - tail, :] = v_scr[0, tail:k, :] + v_scr[1, 0:1, :]
        for r, (m, f) in enumerate(_topk_axis0(cand_scr[...], k, code)):
            c_scr[r:r + 1, :] = m
            f_scr[r:r + 1, :] = f
        cs, fl = c_scr[...], f_scr[...]
        fi = jnp.floor(fl * (1.0 / k))
        fj = fl - fi * k
        a = jnp.zeros_like(fl)
        b = jnp.zeros_like(fl)
        for r in range(k):
            a = jnp.where(fi == float(r), i_scr[0, r:r + 1, :], a)
            b = jnp.where(fj == float(r), i_scr[1, r:r + 1, :], b)
        e = jnp.exp(cs - cs[0:1, :])
        sa_scr[h * k:(h + 1) * k, :] = a
        sb_scr[h * k:(h + 1) * k, :] = b
        sg_scr[h * k:(h + 1) * k, :] = e / jnp.sum(e, axis=0, keepdims=True)
    n_groups = tt // group
    early = PEER_HEADS // 2
    h = 0
    for tg in range(n_groups):
        build_group(tg)
        if h < early and (tg + 1) * early >= (h + 1) * n_groups:
            topk_head(h)
            h += 1
    while h < early:
        topk_head(h)
        h += 1
    blk = nk // (PEER_HEADS - early)
    for q in range(PEER_HEADS - early):
        convert_blocks(q * blk, (q + 1) * blk)
        topk_head(early + q)
    at_scr[...] = sa_scr[...].T
    bt_scr[...] = sb_scr[...].T
    gt_scr[...] = sg_scr[...].T


def _peer_route(st, tt, group):
    nkk, _, bt = st.shape
    k = PEER_TOPK
    nk = PEER_NKEYS
    ns = PEER_HEADS * k
    nt = bt // tt
    groups, tail = _pair_groups(k)
    n_cand = sum(rows for _, _, rows in groups) + k - tail
    pitch = tt + SUBLANES
    return pl.pallas_call(
        functools.partial(_peer_route_kernel, tt=tt, pitch=pitch, group=group),
        grid=(nt + 1,),
        in_specs=[pl.BlockSpec((nkk, nk, tt), lambda i: (0, 0, jnp.minimum(i, nt - 1)))],
        out_specs=pl.BlockSpec((nk, tt, nk), lambda i: (0, jnp.maximum(i - 1, 0), 0)),
        out_shape=jax.ShapeDtypeStruct((nk, bt, nk), BF16),
        scratch_shapes=[pltpu.VMEM((2, k, tt), F32), pltpu.VMEM((2, k, tt), F32),
                        pltpu.VMEM((n_cand, tt), F32),
                        pltpu.VMEM((k, tt), F32), pltpu.VMEM((k, tt), F32),
                        pltpu.VMEM((ns, tt), F32), pltpu.VMEM((ns, tt), F32), pltpu.VMEM((ns, tt), F32),
                        pltpu.VMEM((tt, ns), F32), pltpu.VMEM((tt, ns), F32), pltpu.VMEM((tt, ns), F32),
                        pltpu.VMEM((nk * pitch, nk), F32)],
        compiler_params=_cparams(("arbitrary",)),
        name="peer_route",
    )(st)


def _peer_ffn_kernel(h2_ref, ue_ref, uo_ref, ve_ref, vo_ref, we_ref, wo_ref, h_ref, gf_ref, o_ref,
                     acc_scr, act0, act1):
    j = pl.program_id(1)
    last = pl.num_programs(1) - 1

    @pl.when(j == 0)
    def _():
        acc_scr[...] = jnp.zeros(acc_scr.shape, F32)
        act1[...] = jnp.zeros(act1.shape, F32)

    def consume(act_ref, w_ref, v_ref):
        z = [(jax.nn.gelu(act_ref[:, c * LANES:(c + 1) * LANES]) * w_ref[c].astype(F32)).astype(BF16)
             for c in range(w_ref.shape[0])]
        acc_scr[...] += _dot(jnp.concatenate(z, axis=1), v_ref[...])

    h2 = h2_ref[...]
    act0[...] = _dot_nt(h2, ue_ref[...])
    consume(act1, we_ref, ve_ref)

    @pl.when(j < last)
    def _():
        act1[...] = _dot_nt(h2, uo_ref[...])
        consume(act0, wo_ref, vo_ref)

    @pl.when(j == last)
    def _():
        h = h_ref[...] + acc_scr[...]
        ms = jnp.mean(h * h, axis=-1, keepdims=True)
        o_ref[...] = (h * lax.rsqrt(ms + EPS)) * gf_ref[...]


def _peer_ffn(h2, u, v, w, h, gf, tm, te):
    bt, d = h.shape
    ne = u.shape[0] // te
    assert ne % 2 == 0
    nb = te // LANES
    top = ne - 1
    produce_e = lambda i, j: (jnp.minimum(2 * j, top), 0)
    produce_o = lambda i, j: (jnp.minimum(2 * j + 1, top), 0)
    consume_e = lambda i, j: (jnp.maximum(2 * j - 1, 0), 0)
    consume_o = lambda i, j: (jnp.minimum(2 * j, top), 0)
    return pl.pallas_call(
        _peer_ffn_kernel,
        grid=(bt // tm, ne // 2 + 1),
        in_specs=[pl.BlockSpec((tm, d), lambda i, j: (i, 0)),
                  pl.BlockSpec((te, d), produce_e),
                  pl.BlockSpec((te, d), produce_o),
                  pl.BlockSpec((te, d), consume_e),
                  pl.BlockSpec((te, d), consume_o),
                  pl.BlockSpec((nb, tm, LANES), lambda i, j: (jnp.maximum(2 * j - 1, 0), i, 0)),
                  pl.BlockSpec((nb, tm, LANES), lambda i, j: (jnp.minimum(2 * j, top), i, 0)),
                  pl.BlockSpec((tm, d), lambda i, j: (i, 0)),
                  pl.BlockSpec((1, d), lambda i, j: (0, 0))],
        out_specs=pl.BlockSpec((tm, d), lambda i, j: (i, 0)),
        out_shape=jax.ShapeDtypeStruct((bt, d), F32),
        scratch_shapes=[pltpu.VMEM((tm, d), F32), pltpu.VMEM((tm, te), F32), pltpu.VMEM((tm, te), F32)],
        compiler_params=_cparams(("parallel", "arbitrary")),
        name="peer_ffn",
    )(h2, u, u, v, v, w, w, h, gf)


def _split_offsets():
    hd = HEAD_DIM
    sizes = [NSA_HEADS * hd, NSA_KV * hd, NSA_KV * hd, NSA_KV * hd, NSA_KV * hd, NSA_KV * hd,
             NSA_KV * hd, NSA_HEADS * 3, SWA_HEADS * hd, hd, hd]
    return [0] + [int(c) for c in np.cumsum(sizes)]


def _pick(n, prefs):
    for p in prefs:
        if n % p == 0:
            return p
    return n


def kernel(x, ln1_g, w_in, cmp_pe_k, cmp_w1_k, cmp_w2_k, cmp_pe_v, cmp_w1_v, cmp_w2_v, swa_sinks, w_out, ln2_g, peer_wq, peer_keys, peer_u, peer_v, lnf_g):
    batch, seq, d = x.shape
    bt = batch * seq
    hd = HEAD_DIM
    nsa_tq, nsa_tk = 256, 256
    assert ln1_g.shape[0] == 1, "single layer"
    assert seq % 512 == 0
    slopes = jnp.asarray((2.0 ** (-8.0 * (np.arange(N_HEADS) + 1) / N_HEADS)).astype(np.float32))

    off = _split_offsets()
    w = w_in[0]
    col = lambda k: w[:, off[k]:off[k + 1]]
    scale = hd ** -0.5
    z64 = jnp.zeros((d, hd), F32)
    qn, qs = col(0) * (scale * LOG2E), col(8) * scale
    qn_exp = []
    for h in range(NSA_HEADS):
        qh = qn[:, h * hd:(h + 1) * hd]
        qn_exp += [qh, z64] if h // NSA_GROUP == 0 else [z64, qh]
    qs_exp = []
    for h in range(SWA_HEADS):
        qs_exp += [qs[:, h * hd:(h + 1) * hd], z64]
    gt = col(7)
    gpad = jnp.zeros((d, LANES - NSA_GROUP * 3), F32)
    gcols = []
    for g in range(NSA_KV):
        gcols += [gt[:, g * NSA_GROUP * 3:(g + 1) * NSA_GROUP * 3], gpad]
    w_all = jnp.concatenate(qn_exp + qs_exp + [col(3), col(5), col(9), col(10)]
                            + [col(4), col(6)] + [col(1), col(2)] + gcols, axis=1).astype(BF16)
    swa_q_col = NSA_HEADS * LANES
    nsa_k_col = swa_q_col + SWA_HEADS * LANES
    swa_kv_col = nsa_k_col + 2 * LANES
    n_bf = swa_kv_col + LANES
    n_vt = 2 * LANES

    x2 = x.reshape(bt, d)
    pbf, vt3, pf = _proj(x2, ln1_g[0][None, :], w_all, n_bf, n_vt, 512, nsa_tk)

    ncp = seq // NSA_CMP_STRIDE
    pe_dup = jnp.stack([jnp.concatenate([pe, pe], axis=1) for pe in (cmp_pe_k[0], cmp_pe_v[0])])
    w1_dup = jnp.stack([jnp.concatenate([w1.reshape(NSA_CMP_LEN, hd, -1)] * 2, axis=1)
                        .reshape(2, NSA_CMP_STRIDE * LANES, -1)
                        for w1 in (cmp_w1_k[0], cmp_w1_v[0])]).astype(BF16)
    zc = jnp.zeros((NSA_CMP_HIDDEN, hd), F32)
    w2e = jnp.stack([jnp.stack([jnp.concatenate([w2, zc], axis=1), jnp.concatenate([zc, w2], axis=1)])
                     for w2 in (cmp_w2_k[0], cmp_w2_v[0])]).astype(BF16)
    kcmp, kcmpt = _compress(pf, pe_dup, w1_dup, w2e, batch, seq)

    n_sel = seq // NSA_SEL_BLOCK
    c0 = np.arange(ncp)[None, :] * NSA_CMP_STRIDE
    s0 = np.arange(n_sel)[:, None] * NSA_SEL_BLOCK
    ovt = ((c0 < s0 + NSA_SEL_BLOCK) & (c0 + NSA_CMP_LEN > s0) & (np.arange(ncp)[None, :] < ncp - 1))
    ovt = jnp.asarray(ovt.astype(np.float32), BF16)
    grp = (np.arange(n_sel)[None, :] * NSA_SEL_BLOCK // nsa_tk) == np.arange(seq // nsa_tk)[:, None]
    grp = jnp.asarray(grp.astype(np.float32), BF16)

    o_n = _nsa(slopes, pbf, vt3, pf, kcmp, kcmpt, ovt, grp, batch, seq, nsa_tq, nsa_tk, nsa_k_col)
    o_s = _swa(slopes, swa_sinks[0], pbf, batch, seq, 128, swa_q_col, swa_kv_col)

    keys = peer_keys[0].reshape(2 * PEER_HEADS, PEER_NKEYS, -1).astype(BF16)
    h, h2, st = _mid(x2, o_n, o_s, w_out[0].astype(BF16), ln2_g[0][None, :],
                     peer_wq[0].astype(BF16), keys, _pick(bt, (256, 128)))
    wd = _peer_route(st, 128, 32)
    out = _peer_ffn(h2, peer_u[0].astype(BF16), peer_v[0].astype(BF16), wd, h, lnf_g[None, :],
                    _pick(bt, (1024, 512, 256, 128)), 512)
    return out.reshape(batch, seq, d)
```

```python
import functools

import numpy as np
import jax
import jax.numpy as jnp
from jax import lax
from jax.experimental import pallas as pl
from jax.experimental.pallas import tpu as pltpu

F32 = jnp.float32
BF16 = jnp.bfloat16

HEAD_DIM = 64
N_HEADS = 16
NSA_HEADS = 8
NSA_KV = 2
NSA_GROUP = 4
SWA_HEADS = 8
NSA_CMP_LEN = 32
NSA_CMP_STRIDE = 16
NSA_CMP_HIDDEN = 256
NSA_SEL_BLOCK = 64
NSA_TOPN = 16
NSA_WINDOW = 512
SWA_WINDOW = 128
FORCE_SCORE = 1.0e4
PEER_HEADS = 8
PEER_NKEYS = 128
PEER_TOPK = 16
EPS = 1e-6
NEG = -1.0e30
LANES = 128
SUBLANES = 8
BF16_ROWS = 16
VT_ROWS = HEAD_DIM + BF16_ROWS
LOG2E = 1.4426950408889634
GELU_C1 = 0.7978845608028654
GELU_C2 = 0.044715 * GELU_C1
VMEM_LIMIT = 56 * 1024 * 1024


def _dot(a, b):
    return jnp.dot(a, b, preferred_element_type=F32)


def _dot_nt(a, b):
    return lax.dot_general(a, b, (((1,), (1,)), ((), ())), preferred_element_type=F32)


def _cparams(sem):
    return pltpu.CompilerParams(dimension_semantics=sem, vmem_limit_bytes=VMEM_LIMIT)


def _topk_axis0(v, k, code):
    big = 3.0e38
    out = []
    for _ in range(k):
        m = jnp.max(v, axis=0, keepdims=True)
        c = jnp.min(jnp.where(v == m, code, big), axis=0, keepdims=True)
        out.append((m, c))
        v = jnp.where(code == c, NEG, v)
    return out


def _proj_kernel(x_ref, g_ref, w_ref, obf_ref, vt_ref, of_ref, *, n_bf, n_vt, tk):
    x = x_ref[...]
    ms = jnp.mean(x * x, axis=-1, keepdims=True)
    a = ((x * lax.rsqrt(ms + EPS)) * g_ref[...]).astype(BF16)
    for c0 in range(0, n_bf, 4 * LANES):
        c1 = min(c0 + 4 * LANES, n_bf)
        obf_ref[:, c0:c1] = _dot(a, w_ref[:, c0:c1]).astype(BF16)
    yvt = _dot(a, w_ref[:, n_bf:n_bf + n_vt]).T
    ones = jnp.ones((VT_ROWS - HEAD_DIM, tk), BF16)
    for s in range(vt_ref.shape[0]):
        for c in range(n_vt // HEAD_DIM):
            vt_ref[s, c * VT_ROWS:c * VT_ROWS + HEAD_DIM, :] = (
                yvt[c * HEAD_DIM:(c + 1) * HEAD_DIM, s * tk:(s + 1) * tk].astype(BF16))
            vt_ref[s, c * VT_ROWS + HEAD_DIM:(c + 1) * VT_ROWS, :] = ones
    of_ref[...] = _dot(a, w_ref[:, n_bf + n_vt:])


def _proj(x2, g, w_all, n_bf, n_vt, tm, tk):
    bt, d = x2.shape
    n_all = w_all.shape[1]
    n_f = n_all - n_bf - n_vt
    vt_rows = n_vt // HEAD_DIM * VT_ROWS
    return pl.pallas_call(
        functools.partial(_proj_kernel, n_bf=n_bf, n_vt=n_vt, tk=tk),
        grid=(bt // tm,),
        in_specs=[pl.BlockSpec((tm, d), lambda i: (i, 0)),
                  pl.BlockSpec((1, d), lambda i: (0, 0)),
                  pl.BlockSpec((d, n_all), lambda i: (0, 0))],
        out_specs=[pl.BlockSpec((tm, n_bf), lambda i: (i, 0)),
                   pl.BlockSpec((tm // tk, vt_rows, tk), lambda i: (i, 0, 0)),
                   pl.BlockSpec((tm, n_f), lambda i: (i, 0))],
        out_shape=[jax.ShapeDtypeStruct((bt, n_bf), BF16),
                   jax.ShapeDtypeStruct((bt // tk, vt_rows, tk), BF16),
                   jax.ShapeDtypeStruct((bt, n_f), F32)],
        compiler_params=_cparams(("parallel",)),
        name="proj",
    )(x2, g, w_all)


def _compress_kernel(c_ref, pe_ref, w1_ref, w2_ref, o_ref, ot_ref, *, ncp):
    lo_half = lax.broadcasted_iota(jnp.int32, (ncp, LANES), 1) < HEAD_DIM
    xa = [[], []]
    xb = [[], []]
    for l in range(NSA_CMP_STRIDE):
        x = c_ref[pl.ds(l, ncp, stride=NSA_CMP_STRIDE), :]
        a = x + pe_ref[l:l + 1, :]
        b = x + pe_ref[NSA_CMP_STRIDE + l:NSA_CMP_STRIDE + l + 1, :]
        xa[0].append(jnp.where(lo_half, a, 0.0).astype(BF16))
        xa[1].append(jnp.where(lo_half, 0.0, a).astype(BF16))
        xb[0].append(jnp.where(lo_half, b, 0.0).astype(BF16))
        xb[1].append(jnp.where(lo_half, 0.0, b).astype(BF16))
    acc = None
    for g in range(NSA_KV):
        ya = _dot(jnp.concatenate(xa[g], axis=1), w1_ref[0])
        yb = _dot(jnp.concatenate(xb[g], axis=1), w1_ref[1])
        hid = ya + pltpu.roll(yb, ncp - 1, 0)
        act = jax.nn.gelu(hid).astype(BF16)
        t = _dot(act, w2_ref[g])
        acc = t if acc is None else acc + t
    o_ref[...] = acc.astype(BF16)
    ot_ref[...] = acc.T.astype(BF16)


def _compress(pf, pe_dup, w1_dup, w2e, batch, seq):
    ncp = seq // NSA_CMP_STRIDE
    kdim = NSA_CMP_STRIDE * LANES
    return pl.pallas_call(
        functools.partial(_compress_kernel, ncp=ncp),
        grid=(2, batch),
        in_specs=[pl.BlockSpec((seq, LANES), lambda s, i: (i, s)),
                  pl.BlockSpec((None, NSA_CMP_LEN, LANES), lambda s, i: (s, 0, 0)),
                  pl.BlockSpec((None, 2, kdim, NSA_CMP_HIDDEN), lambda s, i: (s, 0, 0, 0)),
                  pl.BlockSpec((None, NSA_KV, NSA_CMP_HIDDEN, LANES), lambda s, i: (s, 0, 0, 0))],
        out_specs=[pl.BlockSpec((None, None, ncp, LANES), lambda s, i: (s, i, 0, 0)),
                   pl.BlockSpec((None, None, LANES, ncp), lambda s, i: (s, i, 0, 0))],
        out_shape=[jax.ShapeDtypeStruct((2, batch, ncp, LANES), BF16),
                   jax.ShapeDtypeStruct((2, batch, LANES, ncp), BF16)],
        compiler_params=_cparams(("parallel", "parallel")),
        name="compress",
    )(pf, pe_dup, w1_dup, w2e)


def _nsa_kernel(slopes_ref, q_ref, k_ref, vt_ref, kc_ref, vct_ref, gate_ref, ovt_ref, grp_ref, o_ref,
                qt_scr, oc_scr, bias_scr, m_scr, acc_scr, *, tq, tk, seq, ncp):
    i = pl.program_id(1)
    g = pl.program_id(2)
    q0 = i * tq
    n_sel = seq // NSA_SEL_BLOCK
    n_cmp = ncp - 1
    slope = [slopes_ref[SWA_HEADS + NSA_GROUP * g + hh] * LOG2E for hh in range(NSA_GROUP)]

    kc = kc_ref[...]
    vct = vct_ref[...]
    n_c = lax.broadcasted_iota(jnp.int32, (ncp, tq), 0)
    t_c = q0 + lax.broadcasted_iota(jnp.int32, (ncp, tq), 1)
    dist_c = (t_c - (n_c * NSA_CMP_STRIDE + (NSA_CMP_LEN - 1))).astype(F32)
    valid_c = (dist_c >= 0.0) & (n_c < n_cmp)
    psum = jnp.zeros((ncp, tq), F32)
    qts = []
    for hh in range(NSA_GROUP):
        qts.append(q_ref[:, hh * LANES:(hh + 1) * LANES].astype(F32).T.astype(BF16))
        qt_scr[hh] = qts[hh]
    scs = [_dot(kc, qt) for qt in qts]
    pcs = []
    for hh in range(NSA_GROUP):
        s = jnp.where(valid_c, scs[hh] - slope[hh] * dist_c, NEG)
        m = jnp.max(s, axis=0, keepdims=True)
        e = jnp.where(valid_c, jnp.exp2(s - m), 0.0)
        den = jnp.maximum(jnp.sum(e, axis=0, keepdims=True), 1e-30)
        p = e * (1.0 / den)
        psum = psum + p
        pcs.append(p.astype(BF16))
    for hh in range(NSA_GROUP):
        oc_scr[hh] = _dot(vct, pcs[hh])

    imp = _dot(ovt_ref[...], psum.astype(BF16))
    jb = lax.broadcasted_iota(jnp.int32, (n_sel, tq), 0)
    blk_t = (q0 + lax.broadcasted_iota(jnp.int32, (n_sel, tq), 1)) // NSA_SEL_BLOCK
    forced = ((jb == 0) | (jb == blk_t) | (jb == blk_t - 1)) & (jb <= blk_t)
    imp = jnp.where(forced | (jb > blk_t), NEG, imp)
    jbf = jb.astype(F32)
    sel_t = jnp.where(forced, 1.0, 0.0)
    for m_r, idx_r in _topk_axis0(imp, min(NSA_TOPN, n_sel) - 3, jbf):
        sel_t = jnp.where((jbf == idx_r) & (m_r > 0.5 * NEG), 1.0, sel_t)
    sel = sel_t.astype(BF16)
    nkt = grp_ref.shape[0]
    cnt = jnp.max(_dot(grp_ref[...], sel), axis=1, keepdims=True)
    tid = lax.broadcasted_iota(jnp.int32, (nkt, 1), 0)
    j_sel = jnp.min(jnp.where((cnt > 0.5) & (tid > 0), tid.astype(F32), float(nkt))).astype(jnp.int32)
    only_block0 = jnp.max(cnt[0:1, :]) < 1.5

    m_scr[...] = jnp.full(m_scr.shape, NEG, F32)
    acc_scr[...] = jnp.zeros(acc_scr.shape, F32)
    rel = (lax.broadcasted_iota(jnp.int32, (tk, tq), 1)
           - lax.broadcasted_iota(jnp.int32, (tk, tq), 0)).astype(F32)
    for hh in range(NSA_GROUP):
        bias_scr[hh] = slope[hh] * rel
    e_rel = (lax.broadcasted_iota(jnp.int32, (tk, n_sel), 1)
             - lax.broadcasted_iota(jnp.int32, (tk, n_sel), 0) // NSA_SEL_BLOCK)

    def flash_update(branches, off):
        qks = [[_dot(k_tile, qt_scr[hh]) for hh in range(NSA_GROUP)] for _, k_tile, _, _ in branches]
        ps, alphas = [], []
        for bi, (br, k_tile, valid, _) in enumerate(branches):
            for hh in range(NSA_GROUP):
                r = br * NSA_GROUP + hh
                shift = slope[hh] * off
                s = qks[bi][hh] - bias_scr[hh, 0:k_tile.shape[0], :]
                if valid is not None:
                    s = jnp.where(valid, s, NEG)
                m_old = m_scr[r:r + 1, :]
                m_new = jnp.maximum(m_old, jnp.max(s, axis=0, keepdims=True) - shift)
                alphas.append(jnp.exp2(m_old - m_new))
                ps.append(jnp.exp2(s - (m_new + shift)).astype(BF16))
                m_scr[r:r + 1, :] = m_new
        for bi, (br, _, _, vt) in enumerate(branches):
            for hh in range(NSA_GROUP):
                n = bi * NSA_GROUP + hh
                acc_scr[br, hh] = alphas[n] * acc_scr[br, hh] + _dot(vt, ps[n])

    vrow = pl.multiple_of(g * VT_ROWS, BF16_ROWS)

    def tile(j, with_window):
        k0 = j * tk if isinstance(j, int) else pl.multiple_of(j * tk, tk)
        off = (q0 - k0).astype(F32)
        dist = rel + off
        causal = dist >= 0.0
        expand = (e_rel == k0 // NSA_SEL_BLOCK).astype(BF16)
        branches = [(0, k_ref[pl.ds(k0, tk), 0:LANES], (_dot(expand, sel) > 0.5) & causal,
                     vt_ref[j, pl.ds(vrow, VT_ROWS), :])]
        if with_window:
            branches.append((1, k_ref[pl.ds(k0, tk), LANES:2 * LANES], causal & (dist < float(NSA_WINDOW)),
                             vt_ref[j, pl.ds(NSA_KV * VT_ROWS + vrow, VT_ROWS), :]))
        flash_update(branches, off)

    j_win = jnp.maximum(q0 - (NSA_WINDOW - 1), 0) // tk
    j_lo = jnp.minimum(j_sel, j_win)

    @pl.when((j_lo > 0) & jnp.logical_not(only_block0))
    def _():
        tile(0, False)

    @pl.when((j_lo > 0) & only_block0)
    def _():
        nb = NSA_SEL_BLOCK
        flash_update([(0, k_ref[0:nb, 0:LANES], None, vt_ref[0, pl.ds(vrow, VT_ROWS), 0:nb])], q0.astype(F32))

    def body(j, carry):
        @pl.when(j < j_win)
        def _():
            tile(j, False)

        @pl.when(j >= j_win)
        def _():
            tile(j, True)

        return carry

    lax.fori_loop(j_lo, (q0 + tq + tk - 1) // tk, body, 0)

    gst = jax.nn.sigmoid(gate_ref[...]).T
    r0 = pl.multiple_of(g * HEAD_DIM, HEAD_DIM)
    heads = []
    for hh in range(NSA_GROUP):
        inv_s = 1.0 / acc_scr[0, hh, HEAD_DIM:HEAD_DIM + 1, :]
        inv_w = 1.0 / acc_scr[1, hh, HEAD_DIM:HEAD_DIM + 1, :]
        heads.append(gst[3 * hh:3 * hh + 1, :] * oc_scr[hh, pl.ds(r0, HEAD_DIM), :]
                     + (gst[3 * hh + 1:3 * hh + 2, :] * inv_s) * acc_scr[0, hh, 0:HEAD_DIM, :]
                     + (gst[3 * hh + 2:3 * hh + 3, :] * inv_w) * acc_scr[1, hh, 0:HEAD_DIM, :])
    o_ref[...] = jnp.concatenate(heads, axis=0).T.astype(BF16)


def _nsa(slopes, pbf, vt3, pf, kcmp, kcmpt, ovt, grp, batch, seq, tq, tk, k_col):
    nq = seq // tq
    ncp = kcmp.shape[2]
    n_sel = seq // NSA_SEL_BLOCK
    nkt = seq // tk
    gw = NSA_GROUP * LANES
    return pl.pallas_call(
        functools.partial(_nsa_kernel, tq=tq, tk=tk, seq=seq, ncp=ncp),
        grid=(batch, nq, NSA_KV),
        in_specs=[pl.BlockSpec(memory_space=pltpu.SMEM),
                  pl.BlockSpec((tq, gw), lambda b, i, g: (b * nq + i, g)),
                  pl.BlockSpec((seq, 2 * LANES), lambda b, i, g: (b, k_col // (2 * LANES))),
                  pl.BlockSpec((nkt, vt3.shape[1], tk), lambda b, i, g: (b, 0, 0)),
                  pl.BlockSpec((None, None, ncp, LANES), lambda b, i, g: (0, b, 0, 0)),
                  pl.BlockSpec((None, None, LANES, ncp), lambda b, i, g: (1, b, 0, 0)),
                  pl.BlockSpec((tq, LANES), lambda b, i, g: (b * nq + i, 2 + g)),
                  pl.BlockSpec((n_sel, ncp), lambda b, i, g: (0, 0)),
                  pl.BlockSpec((nkt, n_sel), lambda b, i, g: (0, 0))],
        out_specs=pl.BlockSpec((tq, NSA_GROUP * HEAD_DIM), lambda b, i, g: (b * nq + i, g)),
        out_shape=jax.ShapeDtypeStruct((batch * seq, NSA_HEADS * HEAD_DIM), BF16),
        scratch_shapes=[pltpu.VMEM((NSA_GROUP, LANES, tq), BF16),
                        pltpu.VMEM((NSA_GROUP, LANES, tq), F32),
                        pltpu.VMEM((NSA_GROUP, tk, tq), F32),
                        pltpu.VMEM((2 * NSA_GROUP, tq), F32),
                        pltpu.VMEM((2, NSA_GROUP, VT_ROWS, tq), F32)],
        compiler_params=_cparams(("parallel", "parallel", "arbitrary")),
        name="nsa",
    )(slopes, pbf, pbf, vt3, kcmp, kcmpt, pf, ovt, grp)


def _swa_kernel(slopes_ref, sinks_ref, q_ref, kv_ref, o_ref, *, tq):
    i = pl.program_id(1)
    q0 = i * tq
    kwid = tq + SWA_WINDOW
    start = pl.multiple_of(jnp.maximum(q0 - SWA_WINDOW, 0), SWA_WINDOW)
    kvw = kv_ref[pl.ds(start, kwid), :]
    dist = ((q0 - start) + lax.broadcasted_iota(jnp.int32, (tq, kwid), 0)
            - lax.broadcasted_iota(jnp.int32, (tq, kwid), 1)).astype(F32)
    valid = (dist >= 0.0) & (dist < float(SWA_WINDOW))
    lo_half = lax.broadcasted_iota(jnp.int32, (tq, LANES), 1) < HEAD_DIM
    qks = [_dot_nt(q_ref[:, h * LANES:(h + 1) * LANES], kvw) for h in range(SWA_HEADS)]
    es, dens = [], []
    for h in range(SWA_HEADS):
        s = jnp.where(valid, qks[h] - slopes_ref[h] * dist, NEG)
        sink = sinks_ref[h]
        m = jnp.maximum(jnp.max(s, axis=1, keepdims=True), sink)
        e = jnp.where(valid, jnp.exp(s - m), 0.0)
        dens.append(jnp.sum(e, axis=1, keepdims=True) + jnp.exp(sink - m))
        es.append(e.astype(BF16))
    outs = [_dot(es[h], kvw) / dens[h] for h in range(SWA_HEADS)]
    for pr in range(SWA_HEADS // 2):
        a_lo = pltpu.roll(outs[2 * pr], HEAD_DIM, 1)
        o_ref[:, pr * LANES:(pr + 1) * LANES] = jnp.where(lo_half, a_lo, outs[2 * pr + 1]).astype(BF16)


def _swa(slopes, sinks, pbf, batch, seq, tq, q_col, kv_col):
    nq = seq // tq
    qw = SWA_HEADS * LANES
    return pl.pallas_call(
        functools.partial(_swa_kernel, tq=tq),
        grid=(batch, nq),
        in_specs=[pl.BlockSpec(memory_space=pltpu.SMEM),
                  pl.BlockSpec(memory_space=pltpu.SMEM),
                  pl.BlockSpec((tq, qw), lambda b, i: (b * nq + i, q_col // qw)),
                  pl.BlockSpec((seq, LANES), lambda b, i: (b, kv_col // LANES))],
        out_specs=pl.BlockSpec((tq, SWA_HEADS * HEAD_DIM), lambda b, i: (b * nq + i, 0)),
        out_shape=jax.ShapeDtypeStruct((batch * seq, SWA_HEADS * HEAD_DIM), BF16),
        compiler_params=_cparams(("parallel", "parallel")),
        name="swa",
    )(slopes, sinks, pbf, pbf)


def _mid_kernel(x_ref, on_ref, os_ref, wo_ref, g2_ref, wq_ref, keys_ref,
                h_ref, h2_ref, st_ref):
    half = on_ref.shape[1]
    h = x_ref[...] + _dot(on_ref[...], wo_ref[:half, :]) + _dot(os_ref[...], wo_ref[half:, :])
    h_ref[...] = h
    ms = jnp.mean(h * h, axis=-1, keepdims=True)
    h2 = ((h * lax.rsqrt(ms + EPS)) * g2_ref[...]).astype(BF16)
    h2_ref[...] = h2
    qhs = [_dot(h2, wq_ref[:, 2 * hp * LANES:2 * (hp + 1) * LANES]).astype(BF16)
           for hp in range(PEER_HEADS)]
    for c in range(2 * PEER_HEADS):
        st_ref[c] = _dot_nt(keys_ref[c], qhs[c // 2][:, (c % 2) * LANES:(c % 2 + 1) * LANES])


def _mid(x2, o_n, o_s, wo, g2, wq, keys, tm):
    bt, d = x2.shape
    nk = 2 * PEER_HEADS
    return pl.pallas_call(
        _mid_kernel,
        grid=(bt // tm,),
        in_specs=[pl.BlockSpec((tm, d), lambda i: (i, 0)),
                  pl.BlockSpec((tm, o_n.shape[1]), lambda i: (i, 0)),
                  pl.BlockSpec((tm, o_s.shape[1]), lambda i: (i, 0)),
                  pl.BlockSpec(wo.shape, lambda i: (0, 0)),
                  pl.BlockSpec((1, d), lambda i: (0, 0)),
                  pl.BlockSpec(wq.shape, lambda i: (0, 0)),
                  pl.BlockSpec(keys.shape, lambda i: (0, 0, 0))],
        out_specs=[pl.BlockSpec((tm, d), lambda i: (i, 0)),
                   pl.BlockSpec((tm, d), lambda i: (i, 0)),
                   pl.BlockSpec((nk, PEER_NKEYS, tm), lambda i: (0, 0, i))],
        out_shape=[jax.ShapeDtypeStruct((bt, d), F32),
                   jax.ShapeDtypeStruct((bt, d), BF16),
                   jax.ShapeDtypeStruct((nk, PEER_NKEYS, bt), F32)],
        compiler_params=_cparams(("parallel",)),
        name="mid",
    )(x2, o_n, o_s, wo, g2, wq, keys)


def _pair_groups(k):
    groups = []
    i = 0
    while k // (i + 1) > 1:
        n = k // (i + 1)
        groups.append((i, n, -(-n // SUBLANES) * SUBLANES))
        i += 1
    return groups, i


def _peer_route_kernel(st_ref, w_ref, v_scr, i_scr, cand_scr, c_scr, f_scr, sa_scr, sb_scr, sg_scr,
                       at_scr, bt_scr, gt_scr, s_scr, *, tt, pitch, group):
    k = PEER_TOPK
    nk = PEER_NKEYS

    @pl.when(pl.program_id(0) == 0)
    def _():
        at_scr[...] = jnp.zeros(at_scr.shape, F32)
        bt_scr[...] = jnp.zeros(bt_scr.shape, F32)
        gt_scr[...] = jnp.zeros(gt_scr.shape, F32)

    groups, tail = _pair_groups(k)
    rid128 = lax.broadcasted_iota(jnp.int32, (nk, tt), 0).astype(F32)
    codes, pads = [], []
    for (i, n, rows) in groups:
        j = lax.broadcasted_iota(jnp.int32, (rows, tt), 0)
        codes.append((j + i * k).astype(F32))
        pads.append(j < n)
    jt = lax.broadcasted_iota(jnp.int32, (k - tail, tt), 0)
    codes.append(((jt + tail) * k).astype(F32))
    code = jnp.concatenate(codes, axis=0)
    sub = lax.broadcasted_iota(jnp.int32, (nk, at_scr.shape[1]), 0).astype(F32)

    def build_group(tg):
        xs, ys = [], []
        for u in range(group):
            t = tg * group + u
            xs.append(jnp.where(at_scr[t:t + 1, :] == sub, gt_scr[t:t + 1, :], 0.0).astype(BF16))
            ys.append(jnp.where(bt_scr[t:t + 1, :] == sub, 1.0, 0.0).astype(BF16))
        ws = [_dot_nt(xs[u], ys[u]) for u in range(group)]
        for u in range(group):
            s_scr[pl.ds(tg * group + u, nk, stride=pitch), :] = ws[u]

    def convert_blocks(lo, hi):
        for i1 in range(lo, hi):
            w_ref[i1] = s_scr[i1 * pitch:i1 * pitch + tt, :].astype(BF16)

    def topk_head(h):
        for c in range(2):
            for r, (m, idx) in enumerate(_topk_axis0(st_ref[2 * h + c], k, rid128)):
                v_scr[c, r:r + 1, :] = m
                i_scr[c, r:r + 1, :] = idx
        row = 0
        for gi, (i, n, rows) in enumerate(groups):
            vals = v_scr[0, i:i + 1, :] + v_scr[1, 0:rows, :]
            cand_scr[row:row + rows, :] = jnp.where(pads[gi], vals, NEG)
            row += rows
        cand_scr[row:row + k - tail, :] = v_scr[0, tail:k, :] + v_scr[1, 0:1, :]
        for r, (m, f) in enumerate(_topk_axis0(cand_scr[...], k, code)):
            c_scr[r:r + 1, :] = m
            f_scr[r:r + 1, :] = f
        cs, fl = c_scr[...], f_scr[...]
        fi = jnp.floor(fl * (1.0 / k))
        fj = fl - fi * k
        a = jnp.zeros_like(fl)
        b = jnp.zeros_like(fl)
        for r in range(k):
            a = jnp.where(fi == float(r), i_scr[0, r:r + 1, :], a)
            b = jnp.where(fj == float(r), i_scr[1, r:r + 1, :], b)
        e = jnp.exp(cs - cs[0:1, :])
        sa_scr[h * k:(h + 1) * k, :] = a
        sb_scr[h * k:(h + 1) * k, :] = b
        sg_scr[h * k:(h + 1) * k, :] = 0.5 * (e / jnp.sum(e, axis=0, keepdims=True))
    n_groups = tt // group
    early = PEER_HEADS // 2
    h = 0
    for tg in range(n_groups):
        build_group(tg)
        if h < early and (tg + 1) * early >= (h + 1) * n_groups:
            topk_head(h)
            h += 1
    while h < early:
        topk_head(h)
        h += 1
    blk = nk // (PEER_HEADS - early)
    for q in range(PEER_HEADS - early):
        convert_blocks(q * blk, (q + 1) * blk)
        topk_head(early + q)
    at_scr[...] = sa_scr[...].T
    bt_scr[...] = sb_scr[...].T
    gt_scr[...] = sg_scr[...].T


def _peer_route(st, tt, group):
    nkk, _, bt = st.shape
    k = PEER_TOPK
    nk = PEER_NKEYS
    ns = PEER_HEADS * k
    nt = bt // tt
    groups, tail = _pair_groups(k)
    n_cand = sum(rows for _, _, rows in groups) + k - tail
    pitch = tt + SUBLANES
    return pl.pallas_call(
        functools.partial(_peer_route_kernel, tt=tt, pitch=pitch, group=group),
        grid=(nt + 1,),
        in_specs=[pl.BlockSpec((nkk, nk, tt), lambda i: (0, 0, jnp.minimum(i, nt - 1)))],
        out_specs=pl.BlockSpec((nk, tt, nk), lambda i: (0, jnp.maximum(i - 1, 0), 0)),
        out_shape=jax.ShapeDtypeStruct((nk, bt, nk), BF16),
        scratch_shapes=[pltpu.VMEM((2, k, tt), F32), pltpu.VMEM((2, k, tt), F32),
                        pltpu.VMEM((n_cand, tt), F32),
                        pltpu.VMEM((k, tt), F32), pltpu.VMEM((k, tt), F32),
                        pltpu.VMEM((ns, tt), F32), pltpu.VMEM((ns, tt), F32), pltpu.VMEM((ns, tt), F32),
                        pltpu.VMEM((tt, ns), F32), pltpu.VMEM((tt, ns), F32), pltpu.VMEM((tt, ns), F32),
                        pltpu.VMEM((nk * pitch, nk), F32)],
        compiler_params=_cparams(("arbitrary",)),
        name="peer_route",
    )(st)


def _peer_ffn_kernel(h2_ref, ue_ref, uo_ref, ve_ref, vo_ref, we_ref, wo_ref, h_ref, gf_ref, o_ref,
                     acc_scr, act0, act1):
    j = pl.program_id(1)
    last = pl.num_programs(1) - 1

    @pl.when(j == 0)
    def _():
        acc_scr[...] = jnp.zeros(acc_scr.shape, F32)
        act1[...] = jnp.zeros(act1.shape, F32)

    def consume(act_ref, w_ref, v_ref):
        z = []
        for c in range(w_ref.shape[0]):
            x = act_ref[:, c * LANES:(c + 1) * LANES]
            th = jnp.tanh(x * (GELU_C1 + GELU_C2 * (x * x)))
            z.append((x * th + x).astype(BF16) * w_ref[c])
        acc_scr[...] += _dot(jnp.concatenate(z, axis=1), v_ref[...])

    h2 = h2_ref[...]
    act0[...] = _dot_nt(h2, ue_ref[...])
    consume(act1, we_ref, ve_ref)

    @pl.when(j < last)
    def _():
        act1[...] = _dot_nt(h2, uo_ref[...])
        consume(act0, wo_ref, vo_ref)

    @pl.when(j == last)
    def _():
        h = h_ref[...] + acc_scr[...]
        ms = jnp.mean(h * h, axis=-1, keepdims=True)
        o_ref[...] = (h * lax.rsqrt(ms + EPS)) * gf_ref[...]


def _peer_ffn(h2, u, v, w, h, gf, tm, te):
    bt, d = h.shape
    ne = u.shape[0] // te
    assert ne % 2 == 0
    nb = te // LANES
    top = ne - 1
    produce_e = lambda i, j: (jnp.minimum(2 * j, top), 0)
    produce_o = lambda i, j: (jnp.minimum(2 * j + 1, top), 0)
    consume_e = lambda i, j: (jnp.maximum(2 * j - 1, 0), 0)
    consume_o = lambda i, j: (jnp.minimum(2 * j, top), 0)
    return pl.pallas_call(
        _peer_ffn_kernel,
        grid=(bt // tm, ne // 2 + 1),
        in_specs=[pl.BlockSpec((tm, d), lambda i, j: (i, 0)),
                  pl.BlockSpec((te, d), produce_e),
                  pl.BlockSpec((te, d), produce_o),
                  pl.BlockSpec((te, d), consume_e),
                  pl.BlockSpec((te, d), consume_o),
                  pl.BlockSpec((nb, tm, LANES), lambda i, j: (jnp.maximum(2 * j - 1, 0), i, 0)),
                  pl.BlockSpec((nb, tm, LANES), lambda i, j: (jnp.minimum(2 * j, top), i, 0)),
                  pl.BlockSpec((tm, d), lambda i, j: (i, 0)),
                  pl.BlockSpec((1, d), lambda i, j: (0, 0))],
        out_specs=pl.BlockSpec((tm, d), lambda i, j: (i, 0)),
        out_shape=jax.ShapeDtypeStruct((bt, d), F32),
        scratch_shapes=[pltpu.VMEM((tm, d), F32), pltpu.VMEM((tm, te), F32), pltpu.VMEM((tm, te), F32)],
        compiler_params=_cparams(("parallel", "arbitrary")),
        name="peer_ffn",
    )(h2, u, u, v, v, w, w, h, gf)


def _split_offsets():
    hd = HEAD_DIM
    sizes = [NSA_HEADS * hd, NSA_KV * hd, NSA_KV * hd, NSA_KV * hd, NSA_KV * hd, NSA_KV * hd,
             NSA_KV * hd, NSA_HEADS * 3, SWA_HEADS * hd, hd, hd]
    return [0] + [int(c) for c in np.cumsum(sizes)]


def _pick(n, prefs):
    for p in prefs:
        if n % p == 0:
            return p
    return n


def kernel(x, ln1_g, w_in, cmp_pe_k, cmp_w1_k, cmp_w2_k, cmp_pe_v, cmp_w1_v, cmp_w2_v, swa_sinks, w_out, ln2_g, peer_wq, peer_keys, peer_u, peer_v, lnf_g):
    batch, seq, d = x.shape
    bt = batch * seq
    hd = HEAD_DIM
    nsa_tq, nsa_tk = 256, 256
    assert ln1_g.shape[0] == 1, "single layer"
    assert seq % 512 == 0
    slopes = jnp.asarray((2.0 ** (-8.0 * (np.arange(N_HEADS) + 1) / N_HEADS)).astype(np.float32))

    off = _split_offsets()
    w = w_in[0]
    col = lambda k: w[:, off[k]:off[k + 1]]
    scale = hd ** -0.5
    z64 = jnp.zeros((d, hd), F32)
    qn, qs = col(0) * (scale * LOG2E), col(8) * scale
    qn_exp = []
    for h in range(NSA_HEADS):
        qh = qn[:, h * hd:(h + 1) * hd]
        qn_exp += [qh, z64] if h // NSA_GROUP == 0 else [z64, qh]
    qs_exp = []
    for h in range(SWA_HEADS):
        qs_exp += [qs[:, h * hd:(h + 1) * hd], z64]
    gt = col(7)
    gpad = jnp.zeros((d, LANES - NSA_GROUP * 3), F32)
    gcols = []
    for g in range(NSA_KV):
        gcols += [gt[:, g * NSA_GROUP * 3:(g + 1) * NSA_GROUP * 3], gpad]
    w_all = jnp.concatenate(qn_exp + qs_exp + [col(3), col(5), col(9), col(10)]
                            + [col(4), col(6)] + [col(1), col(2)] + gcols, axis=1).astype(BF16)
    swa_q_col = NSA_HEADS * LANES
    nsa_k_col = swa_q_col + SWA_HEADS * LANES
    swa_kv_col = nsa_k_col + 2 * LANES
    n_bf = swa_kv_col + LANES
    n_vt = 2 * LANES

    x2 = x.reshape(bt, d)
    pbf, vt3, pf = _proj(x2, ln1_g[0][None, :], w_all, n_bf, n_vt, 512, nsa_tk)

    ncp = seq // NSA_CMP_STRIDE
    pe_dup = jnp.stack([jnp.concatenate([pe, pe], axis=1) for pe in (cmp_pe_k[0], cmp_pe_v[0])])
    w1_dup = jnp.stack([jnp.concatenate([w1.reshape(NSA_CMP_LEN, hd, -1)] * 2, axis=1)
                        .reshape(2, NSA_CMP_STRIDE * LANES, -1)
                        for w1 in (cmp_w1_k[0], cmp_w1_v[0])]).astype(BF16)
    zc = jnp.zeros((NSA_CMP_HIDDEN, hd), F32)
    w2e = jnp.stack([jnp.stack([jnp.concatenate([w2, zc], axis=1), jnp.concatenate([zc, w2], axis=1)])
                     for w2 in (cmp_w2_k[0], cmp_w2_v[0])]).astype(BF16)
    kcmp, kcmpt = _compress(pf, pe_dup, w1_dup, w2e, batch, seq)

    n_sel = seq // NSA_SEL_BLOCK
    c0 = np.arange(ncp)[None, :] * NSA_CMP_STRIDE
    s0 = np.arange(n_sel)[:, None] * NSA_SEL_BLOCK
    ovt = ((c0 < s0 + NSA_SEL_BLOCK) & (c0 + NSA_CMP_LEN > s0) & (np.arange(ncp)[None, :] < ncp - 1))
    ovt = jnp.asarray(ovt.astype(np.float32), BF16)
    grp = (np.arange(n_sel)[None, :] * NSA_SEL_BLOCK // nsa_tk) == np.arange(seq // nsa_tk)[:, None]
    grp = jnp.asarray(grp.astype(np.float32), BF16)

    o_n = _nsa(slopes, pbf, vt3, pf, kcmp, kcmpt, ovt, grp, batch, seq, nsa_tq, nsa_tk, nsa_k_col)
    o_s = _swa(slopes, swa_sinks[0], pbf, batch, seq, 128, swa_q_col, swa_kv_col)

    keys = peer_keys[0].reshape(2 * PEER_HEADS, PEER_NKEYS, -1).astype(BF16)
    h, h2, st = _mid(x2, o_n, o_s, w_out[0].astype(BF16), ln2_g[0][None, :],
                     peer_wq[0].astype(BF16), keys, _pick(bt, (256, 128)))
    wd = _peer_route(st, 256, 32)
    out = _peer_ffn(h2, peer_u[0].astype(BF16), peer_v[0].astype(BF16), wd, h, lnf_g[None, :],
                    _pick(bt, (1024, 512, 256, 128)), 512)
    return out.reshape(batch, seq, d)
```

```python
import functools

import numpy as np
import jax
import jax.numpy as jnp
from jax import lax
from jax.experimental import pallas as pl
from jax.experimental.pallas import tpu as pltpu

F32 = jnp.float32
BF16 = jnp.bfloat16

HEAD_DIM = 64
N_HEADS = 16
NSA_HEADS = 8
NSA_KV = 2
NSA_GROUP = 4
SWA_HEADS = 8
NSA_CMP_LEN = 32
NSA_CMP_STRIDE = 16
NSA_CMP_HIDDEN = 256
NSA_SEL_BLOCK = 64
NSA_TOPN = 16
NSA_WINDOW = 512
SWA_WINDOW = 128
FORCE_SCORE = 1.0e4
PEER_HEADS = 8
PEER_NKEYS = 128
PEER_TOPK = 16
EPS = 1e-6
NEG = -1.0e30
LANES = 128
SUBLANES = 8
BF16_ROWS = 16
VT_ROWS = HEAD_DIM + BF16_ROWS
LOG2E = 1.4426950408889634
GELU_C1 = 0.7978845608028654
GELU_C2 = 0.044715 * GELU_C1
VMEM_LIMIT = 56 * 1024 * 1024


def _dot(a, b):
    return jnp.dot(a, b, preferred_element_type=F32)


def _dot_nt(a, b):
    return lax.dot_general(a, b, (((1,), (1,)), ((), ())), preferred_element_type=F32)


def _cparams(sem):
    return pltpu.CompilerParams(dimension_semantics=sem, vmem_limit_bytes=VMEM_LIMIT)


def _topk_axis0(v, k, code):
    big = 3.0e38
    out = []
    for _ in range(k):
        m = jnp.max(v, axis=0, keepdims=True)
        c = jnp.min(jnp.where(v == m, code, big), axis=0, keepdims=True)
        out.append((m, c))
        v = jnp.where(code == c, NEG, v)
    return out


def _proj_kernel(x_ref, g_ref, w_ref, obf_ref, vt_ref, of_ref, *, n_bf, n_vt, tk):
    x = x_ref[...]
    ms = jnp.mean(x * x, axis=-1, keepdims=True)
    a = ((x * lax.rsqrt(ms + EPS)) * g_ref[...]).astype(BF16)
    for c0 in range(0, n_bf, 4 * LANES):
        c1 = min(c0 + 4 * LANES, n_bf)
        obf_ref[:, c0:c1] = _dot(a, w_ref[:, c0:c1]).astype(BF16)
    yvt = _dot(a, w_ref[:, n_bf:n_bf + n_vt]).T
    ones = jnp.ones((VT_ROWS - HEAD_DIM, tk), BF16)
    for s in range(vt_ref.shape[0]):
        for c in range(n_vt // HEAD_DIM):
            vt_ref[s, c * VT_ROWS:c * VT_ROWS + HEAD_DIM, :] = (
                yvt[c * HEAD_DIM:(c + 1) * HEAD_DIM, s * tk:(s + 1) * tk].astype(BF16))
            vt_ref[s, c * VT_ROWS + HEAD_DIM:(c + 1) * VT_ROWS, :] = ones
    of_ref[...] = _dot(a, w_ref[:, n_bf + n_vt:])


def _proj(x2, g, w_all, n_bf, n_vt, tm, tk):
    bt, d = x2.shape
    n_all = w_all.shape[1]
    n_f = n_all - n_bf - n_vt
    vt_rows = n_vt // HEAD_DIM * VT_ROWS
    return pl.pallas_call(
        functools.partial(_proj_kernel, n_bf=n_bf, n_vt=n_vt, tk=tk),
        grid=(bt // tm,),
        in_specs=[pl.BlockSpec((tm, d), lambda i: (i, 0)),
                  pl.BlockSpec((1, d), lambda i: (0, 0)),
                  pl.BlockSpec((d, n_all), lambda i: (0, 0))],
        out_specs=[pl.BlockSpec((tm, n_bf), lambda i: (i, 0)),
                   pl.BlockSpec((tm // tk, vt_rows, tk), lambda i: (i, 0, 0)),
                   pl.BlockSpec((tm, n_f), lambda i: (i, 0))],
        out_shape=[jax.ShapeDtypeStruct((bt, n_bf), BF16),
                   jax.ShapeDtypeStruct((bt // tk, vt_rows, tk), BF16),
                   jax.ShapeDtypeStruct((bt, n_f), F32)],
        compiler_params=_cparams(("parallel",)),
        name="proj",
    )(x2, g, w_all)


def _compress_kernel(c_ref, pe_ref, w1_ref, w2_ref, o_ref, ot_ref, *, ncp):
    lo_half = lax.broadcasted_iota(jnp.int32, (ncp, LANES), 1) < HEAD_DIM
    xa = [[], []]
    xb = [[], []]
    for l in range(NSA_CMP_STRIDE):
        x = c_ref[pl.ds(l, ncp, stride=NSA_CMP_STRIDE), :]
        a = x + pe_ref[l:l + 1, :]
        b = x + pe_ref[NSA_CMP_STRIDE + l:NSA_CMP_STRIDE + l + 1, :]
        xa[0].append(jnp.where(lo_half, a, 0.0).astype(BF16))
        xa[1].append(jnp.where(lo_half, 0.0, a).astype(BF16))
        xb[0].append(jnp.where(lo_half, b, 0.0).astype(BF16))
        xb[1].append(jnp.where(lo_half, 0.0, b).astype(BF16))
    acc = None
    for g in range(NSA_KV):
        ya = _dot(jnp.concatenate(xa[g], axis=1), w1_ref[0])
        yb = _dot(jnp.concatenate(xb[g], axis=1), w1_ref[1])
        hid = ya + pltpu.roll(yb, ncp - 1, 0)
        act = jax.nn.gelu(hid).astype(BF16)
        t = _dot(act, w2_ref[g])
        acc = t if acc is None else acc + t
    o_ref[...] = acc.astype(BF16)
    ot_ref[...] = acc.T.astype(BF16)


def _compress(pf, pe_dup, w1_dup, w2e, batch, seq):
    ncp = seq // NSA_CMP_STRIDE
    kdim = NSA_CMP_STRIDE * LANES
    return pl.pallas_call(
        functools.partial(_compress_kernel, ncp=ncp),
        grid=(2, batch),
        in_specs=[pl.BlockSpec((seq, LANES), lambda s, i: (i, s)),
                  pl.BlockSpec((None, NSA_CMP_LEN, LANES), lambda s, i: (s, 0, 0)),
                  pl.BlockSpec((None, 2, kdim, NSA_CMP_HIDDEN), lambda s, i: (s, 0, 0, 0)),
                  pl.BlockSpec((None, NSA_KV, NSA_CMP_HIDDEN, LANES), lambda s, i: (s, 0, 0, 0))],
        out_specs=[pl.BlockSpec((None, None, ncp, LANES), lambda s, i: (s, i, 0, 0)),
                   pl.BlockSpec((None, None, LANES, ncp), lambda s, i: (s, i, 0, 0))],
        out_shape=[jax.ShapeDtypeStruct((2, batch, ncp, LANES), BF16),
                   jax.ShapeDtypeStruct((2, batch, LANES, ncp), BF16)],
        compiler_params=_cparams(("parallel", "parallel")),
        name="compress",
    )(pf, pe_dup, w1_dup, w2e)


def _nsa_kernel(slopes_ref, q_ref, k_ref, vt_ref, kc_ref, vct_ref, gate_ref, ovt_ref, grp_ref, o_ref,
                qt_scr, oc_scr, bias_scr, m_scr, acc_scr, *, tq, tk, seq, ncp):
    i = pl.program_id(1)
    q0 = i * tq
    n_sel = seq // NSA_SEL_BLOCK
    n_cmp = ncp - 1
    nh = NSA_HEADS
    grp_of = lambda h: h // NSA_GROUP
    slope = [slopes_ref[SWA_HEADS + h] * LOG2E for h in range(nh)]

    kc = kc_ref[...]
    vct = vct_ref[...]
    n_c = lax.broadcasted_iota(jnp.int32, (ncp, tq), 0)
    t_c = q0 + lax.broadcasted_iota(jnp.int32, (ncp, tq), 1)
    dist_c = (t_c - (n_c * NSA_CMP_STRIDE + (NSA_CMP_LEN - 1))).astype(F32)
    valid_c = (dist_c >= 0.0) & (n_c < n_cmp)
    qts = []
    for h in range(nh):
        qts.append(q_ref[:, h * LANES:(h + 1) * LANES].astype(F32).T.astype(BF16))
        qt_scr[h] = qts[h]
    scs = [_dot(kc, qt) for qt in qts]
    psum = [jnp.zeros((ncp, tq), F32) for _ in range(NSA_KV)]
    pcs = []
    for h in range(nh):
        s = jnp.where(valid_c, scs[h] - slope[h] * dist_c, NEG)
        m = jnp.max(s, axis=0, keepdims=True)
        e = jnp.where(valid_c, jnp.exp2(s - m), 0.0)
        den = jnp.maximum(jnp.sum(e, axis=0, keepdims=True), 1e-30)
        p = e * (1.0 / den)
        psum[grp_of(h)] = psum[grp_of(h)] + p
        pcs.append(p.astype(BF16))
    for h in range(nh):
        oc_scr[h] = _dot(vct, pcs[h])

    jb = lax.broadcasted_iota(jnp.int32, (n_sel, tq), 0)
    blk_t = (q0 + lax.broadcasted_iota(jnp.int32, (n_sel, tq), 1)) // NSA_SEL_BLOCK
    forced = ((jb == 0) | (jb == blk_t) | (jb == blk_t - 1)) & (jb <= blk_t)
    jbf = jb.astype(F32)
    nkt = grp_ref.shape[0]
    tid = lax.broadcasted_iota(jnp.int32, (nkt, 1), 0)
    sel, j_sel, only_block0 = [], None, None
    for g in range(NSA_KV):
        imp = _dot(ovt_ref[...], psum[g].astype(BF16))
        imp = jnp.where(forced | (jb > blk_t), NEG, imp)
        sel_t = jnp.where(forced, 1.0, 0.0)
        for m_r, idx_r in _topk_axis0(imp, min(NSA_TOPN, n_sel) - 3, jbf):
            sel_t = jnp.where((jbf == idx_r) & (m_r > 0.5 * NEG), 1.0, sel_t)
        sel.append(sel_t.astype(BF16))
        cnt = jnp.max(_dot(grp_ref[...], sel[g]), axis=1, keepdims=True)
        j_g = jnp.min(jnp.where((cnt > 0.5) & (tid > 0), tid.astype(F32), float(nkt))).astype(jnp.int32)
        b0_g = jnp.max(cnt[0:1, :]) < 1.5
        j_sel = j_g if j_sel is None else jnp.minimum(j_sel, j_g)
        only_block0 = b0_g if only_block0 is None else (only_block0 & b0_g)

    m_scr[...] = jnp.full(m_scr.shape, NEG, F32)
    acc_scr[...] = jnp.zeros(acc_scr.shape, F32)
    rel = (lax.broadcasted_iota(jnp.int32, (tk, tq), 1)
           - lax.broadcasted_iota(jnp.int32, (tk, tq), 0)).astype(F32)
    for h in range(nh):
        bias_scr[h] = slope[h] * rel
    e_rel = (lax.broadcasted_iota(jnp.int32, (tk, n_sel), 1)
             - lax.broadcasted_iota(jnp.int32, (tk, n_sel), 0) // NSA_SEL_BLOCK)

    def flash_update(branches, off):
        qks = [[_dot(k_tile, qt_scr[h]) for h in range(nh)] for _, k_tile, _, _ in branches]
        ps, alphas = [], []
        for bi, (br, k_tile, valid, _) in enumerate(branches):
            for h in range(nh):
                r = br * nh + h
                shift = slope[h] * off
                s = qks[bi][h] - bias_scr[h, 0:k_tile.shape[0], :]
                if valid is not None:
                    s = jnp.where(valid[grp_of(h)], s, NEG)
                m_old = m_scr[r:r + 1, :]
                m_new = jnp.maximum(m_old, jnp.max(s, axis=0, keepdims=True) - shift)
                alphas.append(jnp.exp2(m_old - m_new))
                ps.append(jnp.exp2(s - (m_new + shift)).astype(BF16))
                m_scr[r:r + 1, :] = m_new
        for bi, (br, _, _, vts) in enumerate(branches):
            for h in range(nh):
                n = bi * nh + h
                acc_scr[br, h] = alphas[n] * acc_scr[br, h] + _dot(vts[grp_of(h)], ps[n])

    def tile(j, with_window):
        k0 = j * tk if isinstance(j, int) else pl.multiple_of(j * tk, tk)
        off = (q0 - k0).astype(F32)
        dist = rel + off
        causal = dist >= 0.0
        expand = (e_rel == k0 // NSA_SEL_BLOCK).astype(BF16)
        branches = [(0, k_ref[pl.ds(k0, tk), 0:LANES],
                     [(_dot(expand, sel[g]) > 0.5) & causal for g in range(NSA_KV)],
                     [vt_ref[j, g * VT_ROWS:(g + 1) * VT_ROWS, :] for g in range(NSA_KV)])]
        if with_window:
            valid_w = causal & (dist < float(NSA_WINDOW))
            branches.append((1, k_ref[pl.ds(k0, tk), LANES:2 * LANES], [valid_w] * NSA_KV,
                             [vt_ref[j, (NSA_KV + g) * VT_ROWS:(NSA_KV + g + 1) * VT_ROWS, :]
                              for g in range(NSA_KV)]))
        flash_update(branches, off)

    j_win = jnp.maximum(q0 - (NSA_WINDOW - 1), 0) // tk
    j_lo = jnp.minimum(j_sel, j_win)

    @pl.when((j_lo > 0) & jnp.logical_not(only_block0))
    def _():
        tile(0, False)

    @pl.when((j_lo > 0) & only_block0)
    def _():
        nb = NSA_SEL_BLOCK
        flash_update([(0, k_ref[0:nb, 0:LANES], None,
                       [vt_ref[0, g * VT_ROWS:(g + 1) * VT_ROWS, 0:nb] for g in range(NSA_KV)])], q0.astype(F32))

    def body(j, carry):
        @pl.when(j < j_win)
        def _():
            tile(j, False)

        @pl.when(j >= j_win)
        def _():
            tile(j, True)

        return carry

    lax.fori_loop(j_lo, (q0 + tq + tk - 1) // tk, body, 0)

    gst = jax.nn.sigmoid(gate_ref[...]).T
    heads = []
    for h in range(nh):
        g, hh = grp_of(h), h % NSA_GROUP
        gr = g * LANES + 3 * hh
        inv_s = 1.0 / acc_scr[0, h, HEAD_DIM:HEAD_DIM + 1, :]
        inv_w = 1.0 / acc_scr[1, h, HEAD_DIM:HEAD_DIM + 1, :]
        heads.append(gst[gr:gr + 1, :] * oc_scr[h, g * HEAD_DIM:(g + 1) * HEAD_DIM, :]
                     + (gst[gr + 1:gr + 2, :] * inv_s) * acc_scr[0, h, 0:HEAD_DIM, :]
                     + (gst[gr + 2:gr + 3, :] * inv_w) * acc_scr[1, h, 0:HEAD_DIM, :])
    o_ref[...] = jnp.concatenate(heads, axis=0).T.astype(BF16)


def _nsa(slopes, pbf, vt3, pf, kcmp, kcmpt, ovt, grp, batch, seq, tq, tk, k_col):
    nq = seq // tq
    ncp = kcmp.shape[2]
    n_sel = seq // NSA_SEL_BLOCK
    nkt = seq // tk
    qw = NSA_HEADS * LANES
    return pl.pallas_call(
        functools.partial(_nsa_kernel, tq=tq, tk=tk, seq=seq, ncp=ncp),
        grid=(batch, nq),
        in_specs=[pl.BlockSpec(memory_space=pltpu.SMEM),
                  pl.BlockSpec((tq, qw), lambda b, i: (b * nq + i, 0)),
                  pl.BlockSpec((seq, 2 * LANES), lambda b, i: (b, k_col // (2 * LANES))),
                  pl.BlockSpec((nkt, vt3.shape[1], tk), lambda b, i: (b, 0, 0)),
                  pl.BlockSpec((None, None, ncp, LANES), lambda b, i: (0, b, 0, 0)),
                  pl.BlockSpec((None, None, LANES, ncp), lambda b, i: (1, b, 0, 0)),
                  pl.BlockSpec((tq, NSA_KV * LANES), lambda b, i: (b * nq + i, 1)),
                  pl.BlockSpec((n_sel, ncp), lambda b, i: (0, 0)),
                  pl.BlockSpec((nkt, n_sel), lambda b, i: (0, 0))],
        out_specs=pl.BlockSpec((tq, NSA_HEADS * HEAD_DIM), lambda b, i: (b * nq + i, 0)),
        out_shape=jax.ShapeDtypeStruct((batch * seq, NSA_HEADS * HEAD_DIM), BF16),
        scratch_shapes=[pltpu.VMEM((NSA_HEADS, LANES, tq), BF16),
                        pltpu.VMEM((NSA_HEADS, LANES, tq), F32),
                        pltpu.VMEM((NSA_HEADS, tk, tq), F32),
                        pltpu.VMEM((2 * NSA_HEADS, tq), F32),
                        pltpu.VMEM((2, NSA_HEADS, VT_ROWS, tq), F32)],
        compiler_params=_cparams(("parallel", "arbitrary")),
        name="nsa",
    )(slopes, pbf, pbf, vt3, kcmp, kcmpt, pf, ovt, grp)


def _swa_kernel(slopes_ref, sinks_ref, q_ref, kv_ref, o_ref, *, tq):
    i = pl.program_id(1)
    q0 = i * tq
    kwid = tq + SWA_WINDOW
    start = pl.multiple_of(jnp.maximum(q0 - SWA_WINDOW, 0), SWA_WINDOW)
    kvw = kv_ref[pl.ds(start, kwid), :]
    dist = ((q0 - start) + lax.broadcasted_iota(jnp.int32, (tq, kwid), 0)
            - lax.broadcasted_iota(jnp.int32, (tq, kwid), 1)).astype(F32)
    valid = (dist >= 0.0) & (dist < float(SWA_WINDOW))
    lo_half = lax.broadcasted_iota(jnp.int32, (tq, LANES), 1) < HEAD_DIM
    qks = [_dot_nt(q_ref[:, h * LANES:(h + 1) * LANES], kvw) for h in range(SWA_HEADS)]
    es, dens = [], []
    for h in range(SWA_HEADS):
        s = jnp.where(valid, qks[h] - slopes_ref[h] * dist, NEG)
        sink = sinks_ref[h]
        m = jnp.maximum(jnp.max(s, axis=1, keepdims=True), sink)
        e = jnp.where(valid, jnp.exp(s - m), 0.0)
        dens.append(jnp.sum(e, axis=1, keepdims=True) + jnp.exp(sink - m))
        es.append(e.astype(BF16))
    outs = [_dot(es[h], kvw) / dens[h] for h in range(SWA_HEADS)]
    for pr in range(SWA_HEADS // 2):
        a_lo = pltpu.roll(outs[2 * pr], HEAD_DIM, 1)
        o_ref[:, pr * LANES:(pr + 1) * LANES] = jnp.where(lo_half, a_lo, outs[2 * pr + 1]).astype(BF16)


def _swa(slopes, sinks, pbf, batch, seq, tq, q_col, kv_col):
    nq = seq // tq
    qw = SWA_HEADS * LANES
    return pl.pallas_call(
        functools.partial(_swa_kernel, tq=tq),
        grid=(batch, nq),
        in_specs=[pl.BlockSpec(memory_space=pltpu.SMEM),
                  pl.BlockSpec(memory_space=pltpu.SMEM),
                  pl.BlockSpec((tq, qw), lambda b, i: (b * nq + i, q_col // qw)),
                  pl.BlockSpec((seq, LANES), lambda b, i: (b, kv_col // LANES))],
        out_specs=pl.BlockSpec((tq, SWA_HEADS * HEAD_DIM), lambda b, i: (b * nq + i, 0)),
        out_shape=jax.ShapeDtypeStruct((batch * seq, SWA_HEADS * HEAD_DIM), BF16),
        compiler_params=_cparams(("parallel", "parallel")),
        name="swa",
    )(slopes, sinks, pbf, pbf)


def _mid_kernel(x_ref, on_ref, os_ref, wo_ref, g2_ref, wq_ref, keys_ref,
                h_ref, h2_ref, st_ref):
    half = on_ref.shape[1]
    h = x_ref[...] + _dot(on_ref[...], wo_ref[:half, :]) + _dot(os_ref[...], wo_ref[half:, :])
    h_ref[...] = h
    ms = jnp.mean(h * h, axis=-1, keepdims=True)
    h2 = ((h * lax.rsqrt(ms + EPS)) * g2_ref[...]).astype(BF16)
    h2_ref[...] = h2
    qhs = [_dot(h2, wq_ref[:, 2 * hp * LANES:2 * (hp + 1) * LANES]).astype(BF16)
           for hp in range(PEER_HEADS)]
    for c in range(2 * PEER_HEADS):
        st_ref[c] = _dot_nt(keys_ref[c], qhs[c // 2][:, (c % 2) * LANES:(c % 2 + 1) * LANES])


def _mid(x2, o_n, o_s, wo, g2, wq, keys, tm):
    bt, d = x2.shape
    nk = 2 * PEER_HEADS
    return pl.pallas_call(
        _mid_kernel,
        grid=(bt // tm,),
        in_specs=[pl.BlockSpec((tm, d), lambda i: (i, 0)),
                  pl.BlockSpec((tm, o_n.shape[1]), lambda i: (i, 0)),
                  pl.BlockSpec((tm, o_s.shape[1]), lambda i: (i, 0)),
                  pl.BlockSpec(wo.shape, lambda i: (0, 0)),
                  pl.BlockSpec((1, d), lambda i: (0, 0)),
                  pl.BlockSpec(wq.shape, lambda i: (0, 0)),
                  pl.BlockSpec(keys.shape, lambda i: (0, 0, 0))],
        out_specs=[pl.BlockSpec((tm, d), lambda i: (i, 0)),
                   pl.BlockSpec((tm, d), lambda i: (i, 0)),
                   pl.BlockSpec((nk, PEER_NKEYS, tm), lambda i: (0, 0, i))],
        out_shape=[jax.ShapeDtypeStruct((bt, d), F32),
                   jax.ShapeDtypeStruct((bt, d), BF16),
                   jax.ShapeDtypeStruct((nk, PEER_NKEYS, bt), F32)],
        compiler_params=_cparams(("parallel",)),
        name="mid",
    )(x2, o_n, o_s, wo, g2, wq, keys)


def _pair_groups(k):
    groups = []
    i = 0
    while k // (i + 1) > 1:
        n = k // (i + 1)
        groups.append((i, n, -(-n // SUBLANES) * SUBLANES))
        i += 1
    return groups, i


def _peer_route_kernel(st_ref, w_ref, v_scr, i_scr, cand_scr, c_scr, f_scr, sa_scr, sb_scr, sg_scr,
                       at_scr, bt_scr, gt_scr, s_scr, *, tt, pitch, group):
    k = PEER_TOPK
    nk = PEER_NKEYS

    @pl.when(pl.program_id(0) == 0)
    def _():
        at_scr[...] = jnp.zeros(at_scr.shape, F32)
        bt_scr[...] = jnp.zeros(bt_scr.shape, F32)
        gt_scr[...] = jnp.zeros(gt_scr.shape, F32)

    groups, tail = _pair_groups(k)
    rid128 = lax.broadcasted_iota(jnp.int32, (nk, tt), 0).astype(F32)
    codes, pads = [], []
    for (i, n, rows) in groups:
        j = lax.broadcasted_iota(jnp.int32, (rows, tt), 0)
        codes.append((j + i * k).astype(F32))
        pads.append(j < n)
    jt = lax.broadcasted_iota(jnp.int32, (k - tail, tt), 0)
    codes.append(((jt + tail) * k).astype(F32))
    code = jnp.concatenate(codes, axis=0)
    sub = lax.broadcasted_iota(jnp.int32, (nk, at_scr.shape[1]), 0).astype(F32)

    def build_group(tg):
        xs, ys = [], []
        for u in range(group):
            t = tg * group + u
            xs.append(jnp.where(at_scr[t:t + 1, :] == sub, gt_scr[t:t + 1, :], 0.0).astype(BF16))
            ys.append(jnp.where(bt_scr[t:t + 1, :] == sub, 1.0, 0.0).astype(BF16))
        ws = [_dot_nt(xs[u], ys[u]) for u in range(group)]
        for u in range(group):
            s_scr[pl.ds(tg * group + u, nk, stride=pitch), :] = ws[u]

    def convert_blocks(lo, hi):
        for i1 in range(lo, hi):
            w_ref[i1] = s_scr[i1 * pitch:i1 * pitch + tt, :].astype(BF16)

    def topk_head(h):
        for c in range(2):
            for r, (m, idx) in enumerate(_topk_axis0(st_ref[2 * h + c], k, rid128)):
                v_scr[c, r:r + 1, :] = m
                i_scr[c, r:r + 1, :] = idx
        row = 0
        for gi, (i, n, rows) in enumerate(groups):
            vals = v_scr[0, i:i + 1, :] + v_scr[1, 0:rows, :]
            cand_scr[row:row + rows, :] = jnp.where(pads[gi], vals, NEG)
            row += rows
        cand_scr[row:row + k - tail, :] = v_scr[0, tail:k, :] + v_scr[1, 0:1, :]
        for r, (m, f) in enumerate(_topk_axis0(cand_scr[...], k, code)):
            c_scr[r:r + 1, :] = m
            f_scr[r:r + 1, :] = f
        cs, fl = c_scr[...], f_scr[...]
        fi = jnp.floor(fl * (1.0 / k))
        fj = fl - fi * k
        a = jnp.zeros_like(fl)
        b = jnp.zeros_like(fl)
        for r in range(k):
            a = jnp.where(fi == float(r), i_scr[0, r:r + 1, :], a)
            b = jnp.where(fj == float(r), i_scr[1, r:r + 1, :], b)
        e = jnp.exp(cs - cs[0:1, :])
        sa_scr[h * k:(h + 1) * k, :] = a
        sb_scr[h * k:(h + 1) * k, :] = b
        sg_scr[h * k:(h + 1) * k, :] = 0.5 * (e / jnp.sum(e, axis=0, keepdims=True))
    n_groups = tt // group
    early = PEER_HEADS // 2
    h = 0
    for tg in range(n_groups):
        build_group(tg)
        if h < early and (tg + 1) * early >= (h + 1) * n_groups:
            topk_head(h)
            h += 1
    while h < early:
        topk_head(h)
        h += 1
    blk = nk // (PEER_HEADS - early)
    for q in range(PEER_HEADS - early):
        convert_blocks(q * blk, (q + 1) * blk)
        topk_head(early + q)
    at_scr[...] = sa_scr[...].T
    bt_scr[...] = sb_scr[...].T
    gt_scr[...] = sg_scr[...].T


def _peer_route(st, tt, group):
    nkk, _, bt = st.shape
    k = PEER_TOPK
    nk = PEER_NKEYS
    ns = PEER_HEADS * k
    nt = bt // tt
    groups, tail = _pair_groups(k)
    n_cand = sum(rows for _, _, rows in groups) + k - tail
    pitch = tt + SUBLANES
    return pl.pallas_call(
        functools.partial(_peer_route_kernel, tt=tt, pitch=pitch, group=group),
        grid=(nt + 1,),
        in_specs=[pl.BlockSpec((nkk, nk, tt), lambda i: (0, 0, jnp.minimum(i, nt - 1)))],
        out_specs=pl.BlockSpec((nk, tt, nk), lambda i: (0, jnp.maximum(i - 1, 0), 0)),
        out_shape=jax.ShapeDtypeStruct((nk, bt, nk), BF16),
        scratch_shapes=[pltpu.VMEM((2, k, tt), F32), pltpu.VMEM((2, k, tt), F32),
                        pltpu.VMEM((n_cand, tt), F32),
                        pltpu.VMEM((k, tt), F32), pltpu.VMEM((k, tt), F32),
                        pltpu.VMEM((ns, tt), F32), pltpu.VMEM((ns, tt), F32), pltpu.VMEM((ns, tt), F32),
                        pltpu.VMEM((tt, ns), F32), pltpu.VMEM((tt, ns), F32), pltpu.VMEM((tt, ns), F32),
                        pltpu.VMEM((nk * pitch, nk), F32)],
        compiler_params=_cparams(("arbitrary",)),
        name="peer_route",
    )(st)


def _peer_ffn_kernel(h2_ref, ue_ref, uo_ref, ve_ref, vo_ref, we_ref, wo_ref, h_ref, gf_ref, o_ref,
                     acc_scr, act0, act1):
    j = pl.program_id(1)
    last = pl.num_programs(1) - 1

    @pl.when(j == 0)
    def _():
        acc_scr[...] = jnp.zeros(acc_scr.shape, F32)
        act1[...] = jnp.zeros(act1.shape, F32)

    def consume(act_ref, w_ref, v_ref):
        z = []
        for c in range(w_ref.shape[0]):
            x = act_ref[:, c * LANES:(c + 1) * LANES]
            th = jnp.tanh(x * (GELU_C1 + GELU_C2 * (x * x)))
            z.append((x * th + x).astype(BF16) * w_ref[c])
        acc_scr[...] += _dot(jnp.concatenate(z, axis=1), v_ref[...])

    h2 = h2_ref[...]
    act0[...] = _dot_nt(h2, ue_ref[...])
    consume(act1, we_ref, ve_ref)

    @pl.when(j < last)
    def _():
        act1[...] = _dot_nt(h2, uo_ref[...])
        consume(act0, wo_ref, vo_ref)

    @pl.when(j == last)
    def _():
        h = h_ref[...] + acc_scr[...]
        ms = jnp.mean(h * h, axis=-1, keepdims=True)
        o_ref[...] = (h * lax.rsqrt(ms + EPS)) * gf_ref[...]


def _peer_ffn(h2, u, v, w, h, gf, tm, te):
    bt, d = h.shape
    ne = u.shape[0] // te
    assert ne % 2 == 0
    nb = te // LANES
    top = ne - 1
    produce_e = lambda i, j: (jnp.minimum(2 * j, top), 0)
    produce_o = lambda i, j: (jnp.minimum(2 * j + 1, top), 0)
    consume_e = lambda i, j: (jnp.maximum(2 * j - 1, 0), 0)
    consume_o = lambda i, j: (jnp.minimum(2 * j, top), 0)
    return pl.pallas_call(
        _peer_ffn_kernel,
        grid=(bt // tm, ne // 2 + 1),
        in_specs=[pl.BlockSpec((tm, d), lambda i, j: (i, 0)),
                  pl.BlockSpec((te, d), produce_e),
                  pl.BlockSpec((te, d), produce_o),
                  pl.BlockSpec((te, d), consume_e),
                  pl.BlockSpec((te, d), consume_o),
                  pl.BlockSpec((nb, tm, LANES), lambda i, j: (jnp.maximum(2 * j - 1, 0), i, 0)),
                  pl.BlockSpec((nb, tm, LANES), lambda i, j: (jnp.minimum(2 * j, top), i, 0)),
                  pl.BlockSpec((tm, d), lambda i, j: (i, 0)),
                  pl.BlockSpec((1, d), lambda i, j: (0, 0))],
        out_specs=pl.BlockSpec((tm, d), lambda i, j: (i, 0)),
        out_shape=jax.ShapeDtypeStruct((bt, d), F32),
        scratch_shapes=[pltpu.VMEM((tm, d), F32), pltpu.VMEM((tm, te), F32), pltpu.VMEM((tm, te), F32)],
        compiler_params=_cparams(("parallel", "arbitrary")),
        name="peer_ffn",
    )(h2, u, u, v, v, w, w, h, gf)


def _split_offsets():
    hd = HEAD_DIM
    sizes = [NSA_HEADS * hd, NSA_KV * hd, NSA_KV * hd, NSA_KV * hd, NSA_KV * hd, NSA_KV * hd,
             NSA_KV * hd, NSA_HEADS * 3, SWA_HEADS * hd, hd, hd]
    return [0] + [int(c) for c in np.cumsum(sizes)]


def _pick(n, prefs):
    for p in prefs:
        if n % p == 0:
            return p
    return n


def kernel(x, ln1_g, w_in, cmp_pe_k, cmp_w1_k, cmp_w2_k, cmp_pe_v, cmp_w1_v, cmp_w2_v, swa_sinks, w_out, ln2_g, peer_wq, peer_keys, peer_u, peer_v, lnf_g):
    batch, seq, d = x.shape
    bt = batch * seq
    hd = HEAD_DIM
    nsa_tq, nsa_tk = 256, 256
    assert ln1_g.shape[0] == 1, "single layer"
    assert seq % 512 == 0
    slopes = jnp.asarray((2.0 ** (-8.0 * (np.arange(N_HEADS) + 1) / N_HEADS)).astype(np.float32))

    off = _split_offsets()
    w = w_in[0]
    col = lambda k: w[:, off[k]:off[k + 1]]
    scale = hd ** -0.5
    z64 = jnp.zeros((d, hd), F32)
    qn, qs = col(0) * (scale * LOG2E), col(8) * scale
    qn_exp = []
    for h in range(NSA_HEADS):
        qh = qn[:, h * hd:(h + 1) * hd]
        qn_exp += [qh, z64] if h // NSA_GROUP == 0 else [z64, qh]
    qs_exp = []
    for h in range(SWA_HEADS):
        qs_exp += [qs[:, h * hd:(h + 1) * hd], z64]
    gt = col(7)
    gpad = jnp.zeros((d, LANES - NSA_GROUP * 3), F32)
    gcols = []
    for g in range(NSA_KV):
        gcols += [gt[:, g * NSA_GROUP * 3:(g + 1) * NSA_GROUP * 3], gpad]
    w_all = jnp.concatenate(qn_exp + qs_exp + [col(3), col(5), col(9), col(10)]
                            + [col(4), col(6)] + [col(1), col(2)] + gcols, axis=1).astype(BF16)
    swa_q_col = NSA_HEADS * LANES
    nsa_k_col = swa_q_col + SWA_HEADS * LANES
    swa_kv_col = nsa_k_col + 2 * LANES
    n_bf = swa_kv_col + LANES
    n_vt = 2 * LANES

    x2 = x.reshape(bt, d)
    pbf, vt3, pf = _proj(x2, ln1_g[0][None, :], w_all, n_bf, n_vt, 512, nsa_tk)

    ncp = seq // NSA_CMP_STRIDE
    pe_dup = jnp.stack([jnp.concatenate([pe, pe], axis=1) for pe in (cmp_pe_k[0], cmp_pe_v[0])])
    w1_dup = jnp.stack([jnp.concatenate([w1.reshape(NSA_CMP_LEN, hd, -1)] * 2, axis=1)
                        .reshape(2, NSA_CMP_STRIDE * LANES, -1)
                        for w1 in (cmp_w1_k[0], cmp_w1_v[0])]).astype(BF16)
    zc = jnp.zeros((NSA_CMP_HIDDEN, hd), F32)
    w2e = jnp.stack([jnp.stack([jnp.concatenate([w2, zc], axis=1), jnp.concatenate([zc, w2], axis=1)])
                     for w2 in (cmp_w2_k[0], cmp_w2_v[0])]).astype(BF16)
    kcmp, kcmpt = _compress(pf, pe_dup, w1_dup, w2e, batch, seq)

    n_sel = seq // NSA_SEL_BLOCK
    c0 = np.arange(ncp)[None, :] * NSA_CMP_STRIDE
    s0 = np.arange(n_sel)[:, None] * NSA_SEL_BLOCK
    ovt = ((c0 < s0 + NSA_SEL_BLOCK) & (c0 + NSA_CMP_LEN > s0) & (np.arange(ncp)[None, :] < ncp - 1))
    ovt = jnp.asarray(ovt.astype(np.float32), BF16)
    grp = (np.arange(n_sel)[None, :] * NSA_SEL_BLOCK // nsa_tk) == np.arange(seq // nsa_tk)[:, None]
    grp = jnp.asarray(grp.astype(np.float32), BF16)

    o_n = _nsa(slopes, pbf, vt3, pf, kcmp, kcmpt, ovt, grp, batch, seq, nsa_tq, nsa_tk, nsa_k_col)
    o_s = _swa(slopes, swa_sinks[0], pbf, batch, seq, 128, swa_q_col, swa_kv_col)

    keys = peer_keys[0].reshape(2 * PEER_HEADS, PEER_NKEYS, -1).astype(BF16)
    h, h2, st = _mid(x2, o_n, o_s, w_out[0].astype(BF16), ln2_g[0][None, :],
                     peer_wq[0].astype(BF16), keys, _pick(bt, (512, 256, 128)))
    wd = _peer_route(st, 256, 32)
    out = _peer_ffn(h2, peer_u[0].astype(BF16), peer_v[0].astype(BF16), wd, h, lnf_g[None, :],
                    _pick(bt, (1024, 512, 256, 128)), 512)
    return out.reshape(batch, seq, d)
```

```python
import functools

import numpy as np
import jax
import jax.numpy as jnp
from jax import lax
from jax.experimental import pallas as pl
from jax.experimental.pallas import tpu as pltpu

F32 = jnp.float32
BF16 = jnp.bfloat16

HEAD_DIM = 64
N_HEADS = 16
NSA_HEADS = 8
NSA_KV = 2
NSA_GROUP = 4
SWA_HEADS = 8
NSA_CMP_LEN = 32
NSA_CMP_STRIDE = 16
NSA_CMP_HIDDEN = 256
NSA_SEL_BLOCK = 64
NSA_TOPN = 16
NSA_WINDOW = 512
SWA_WINDOW = 128
FORCE_SCORE = 1.0e4
PEER_HEADS = 8
PEER_NKEYS = 128
PEER_TOPK = 16
EPS = 1e-6
NEG = -1.0e30
LANES = 128
SUBLANES = 8
BF16_ROWS = 16
VT_ROWS = HEAD_DIM + BF16_ROWS
LOG2E = 1.4426950408889634
GELU_C1 = 0.7978845608028654
GELU_C2 = 0.044715 * GELU_C1
VMEM_LIMIT = 56 * 1024 * 1024


def _dot(a, b):
    return jnp.dot(a, b, preferred_element_type=F32)


def _dot_nt(a, b):
    return lax.dot_general(a, b, (((1,), (1,)), ((), ())), preferred_element_type=F32)


def _cparams(sem):
    return pltpu.CompilerParams(dimension_semantics=sem, vmem_limit_bytes=VMEM_LIMIT)


def _topk_axis0(v, k, code):
    big = 3.0e38
    out = []
    for _ in range(k):
        m = jnp.max(v, axis=0, keepdims=True)
        c = jnp.min(jnp.where(v == m, code, big), axis=0, keepdims=True)
        out.append((m, c))
        v = jnp.where(code == c, NEG, v)
    return out


def _proj_kernel(x_ref, g_ref, w_ref, obf_ref, vt_ref, of_ref, *, n_bf, n_vt, tk):
    x = x_ref[...]
    ms = jnp.mean(x * x, axis=-1, keepdims=True)
    a = ((x * lax.rsqrt(ms + EPS)) * g_ref[...]).astype(BF16)
    for c0 in range(0, n_bf, 4 * LANES):
        c1 = min(c0 + 4 * LANES, n_bf)
        obf_ref[:, c0:c1] = _dot(a, w_ref[:, c0:c1]).astype(BF16)
    yvt = _dot(a, w_ref[:, n_bf:n_bf + n_vt]).T
    ones = jnp.ones((VT_ROWS - HEAD_DIM, tk), BF16)
    for s in range(vt_ref.shape[0]):
        for c in range(n_vt // HEAD_DIM):
            vt_ref[s, c * VT_ROWS:c * VT_ROWS + HEAD_DIM, :] = (
                yvt[c * HEAD_DIM:(c + 1) * HEAD_DIM, s * tk:(s + 1) * tk].astype(BF16))
            vt_ref[s, c * VT_ROWS + HEAD_DIM:(c + 1) * VT_ROWS, :] = ones
    of_ref[...] = _dot(a, w_ref[:, n_bf + n_vt:])


def _proj(x2, g, w_all, n_bf, n_vt, tm, tk):
    bt, d = x2.shape
    n_all = w_all.shape[1]
    n_f = n_all - n_bf - n_vt
    vt_rows = n_vt // HEAD_DIM * VT_ROWS
    return pl.pallas_call(
        functools.partial(_proj_kernel, n_bf=n_bf, n_vt=n_vt, tk=tk),
        grid=(bt // tm,),
        in_specs=[pl.BlockSpec((tm, d), lambda i: (i, 0)),
                  pl.BlockSpec((1, d), lambda i: (0, 0)),
                  pl.BlockSpec((d, n_all), lambda i: (0, 0))],
        out_specs=[pl.BlockSpec((tm, n_bf), lambda i: (i, 0)),
                   pl.BlockSpec((tm // tk, vt_rows, tk), lambda i: (i, 0, 0)),
                   pl.BlockSpec((tm, n_f), lambda i: (i, 0))],
        out_shape=[jax.ShapeDtypeStruct((bt, n_bf), BF16),
                   jax.ShapeDtypeStruct((bt // tk, vt_rows, tk), BF16),
                   jax.ShapeDtypeStruct((bt, n_f), F32)],
        compiler_params=_cparams(("parallel",)),
        name="proj",
    )(x2, g, w_all)


def _compress_kernel(c_ref, pe_ref, w1_ref, w2_ref, o_ref, ot_ref, *, ncp):
    lo_half = lax.broadcasted_iota(jnp.int32, (ncp, LANES), 1) < HEAD_DIM
    xa = [[], []]
    xb = [[], []]
    for l in range(NSA_CMP_STRIDE):
        x = c_ref[pl.ds(l, ncp, stride=NSA_CMP_STRIDE), :]
        a = x + pe_ref[l:l + 1, :]
        b = x + pe_ref[NSA_CMP_STRIDE + l:NSA_CMP_STRIDE + l + 1, :]
        xa[0].append(jnp.where(lo_half, a, 0.0).astype(BF16))
        xa[1].append(jnp.where(lo_half, 0.0, a).astype(BF16))
        xb[0].append(jnp.where(lo_half, b, 0.0).astype(BF16))
        xb[1].append(jnp.where(lo_half, 0.0, b).astype(BF16))
    acc = None
    for g in range(NSA_KV):
        ya = _dot(jnp.concatenate(xa[g], axis=1), w1_ref[0])
        yb = _dot(jnp.concatenate(xb[g], axis=1), w1_ref[1])
        hid = ya + pltpu.roll(yb, ncp - 1, 0)
        act = jax.nn.gelu(hid).astype(BF16)
        t = _dot(act, w2_ref[g])
        acc = t if acc is None else acc + t
    o_ref[...] = acc.astype(BF16)
    ot_ref[...] = acc.T.astype(BF16)


def _compress(pf, pe_dup, w1_dup, w2e, batch, seq):
    ncp = seq // NSA_CMP_STRIDE
    kdim = NSA_CMP_STRIDE * LANES
    return pl.pallas_call(
        functools.partial(_compress_kernel, ncp=ncp),
        grid=(2, batch),
        in_specs=[pl.BlockSpec((seq, LANES), lambda s, i: (i, s)),
                  pl.BlockSpec((None, NSA_CMP_LEN, LANES), lambda s, i: (s, 0, 0)),
                  pl.BlockSpec((None, 2, kdim, NSA_CMP_HIDDEN), lambda s, i: (s, 0, 0, 0)),
                  pl.BlockSpec((None, NSA_KV, NSA_CMP_HIDDEN, LANES), lambda s, i: (s, 0, 0, 0))],
        out_specs=[pl.BlockSpec((None, None, ncp, LANES), lambda s, i: (s, i, 0, 0)),
                   pl.BlockSpec((None, None, LANES, ncp), lambda s, i: (s, i, 0, 0))],
        out_shape=[jax.ShapeDtypeStruct((2, batch, ncp, LANES), BF16),
                   jax.ShapeDtypeStruct((2, batch, LANES, ncp), BF16)],
        compiler_params=_cparams(("parallel", "parallel")),
        name="compress",
    )(pf, pe_dup, w1_dup, w2e)


def _nsa_kernel(slopes_ref, q_ref, k_ref, vt_ref, kc_ref, vct_ref, gate_ref, ovt_ref, grp_ref, o_ref,
                qt_scr, oc_scr, bias_scr, m_scr, acc_scr, *, tq, tk, seq, ncp):
    i = pl.program_id(1)
    q0 = i * tq
    n_sel = seq // NSA_SEL_BLOCK
    n_cmp = ncp - 1
    nh = NSA_HEADS
    grp_of = lambda h: h // NSA_GROUP
    slope = [slopes_ref[SWA_HEADS + h] * LOG2E for h in range(nh)]

    kc = kc_ref[...]
    vct = vct_ref[...]
    n_c = lax.broadcasted_iota(jnp.int32, (ncp, tq), 0)
    t_c = q0 + lax.broadcasted_iota(jnp.int32, (ncp, tq), 1)
    dist_c = (t_c - (n_c * NSA_CMP_STRIDE + (NSA_CMP_LEN - 1))).astype(F32)
    valid_c = (dist_c >= 0.0) & (n_c < n_cmp)
    qts = []
    for h in range(nh):
        qts.append(q_ref[:, h * LANES:(h + 1) * LANES].astype(F32).T.astype(BF16))
        qt_scr[h] = qts[h]
    scs = [_dot(kc, qt) for qt in qts]
    psum = [jnp.zeros((ncp, tq), F32) for _ in range(NSA_KV)]
    pcs = []
    for h in range(nh):
        s = jnp.where(valid_c, scs[h] - slope[h] * dist_c, NEG)
        m = jnp.max(s, axis=0, keepdims=True)
        e = jnp.where(valid_c, jnp.exp2(s - m), 0.0)
        den = jnp.maximum(jnp.sum(e, axis=0, keepdims=True), 1e-30)
        p = e * (1.0 / den)
        psum[grp_of(h)] = psum[grp_of(h)] + p
        pcs.append(p.astype(BF16))
    for h in range(nh):
        oc_scr[h] = _dot(vct, pcs[h])

    jb = lax.broadcasted_iota(jnp.int32, (n_sel, tq), 0)
    blk_t = (q0 + lax.broadcasted_iota(jnp.int32, (n_sel, tq), 1)) // NSA_SEL_BLOCK
    forced = ((jb == 0) | (jb == blk_t) | (jb == blk_t - 1)) & (jb <= blk_t)
    jbf = jb.astype(F32)
    nkt = grp_ref.shape[0]
    tid = lax.broadcasted_iota(jnp.int32, (nkt, 1), 0)
    sel, j_sel, only_block0 = [], None, None
    for g in range(NSA_KV):
        imp = _dot(ovt_ref[...], psum[g].astype(BF16))
        imp = jnp.where(forced | (jb > blk_t), NEG, imp)
        sel_t = jnp.where(forced, 1.0, 0.0)
        for m_r, idx_r in _topk_axis0(imp, min(NSA_TOPN, n_sel) - 3, jbf):
            sel_t = jnp.where((jbf == idx_r) & (m_r > 0.5 * NEG), 1.0, sel_t)
        sel.append(sel_t.astype(BF16))
        cnt = jnp.max(_dot(grp_ref[...], sel[g]), axis=1, keepdims=True)
        j_g = jnp.min(jnp.where((cnt > 0.5) & (tid > 0), tid.astype(F32), float(nkt))).astype(jnp.int32)
        b0_g = jnp.max(cnt[0:1, :]) < 1.5
        j_sel = j_g if j_sel is None else jnp.minimum(j_sel, j_g)
        only_block0 = b0_g if only_block0 is None else (only_block0 & b0_g)

    m_scr[...] = jnp.full(m_scr.shape, NEG, F32)
    acc_scr[...] = jnp.zeros(acc_scr.shape, F32)
    rel = (lax.broadcasted_iota(jnp.int32, (tk, tq), 1)
           - lax.broadcasted_iota(jnp.int32, (tk, tq), 0)).astype(F32)
    for h in range(nh):
        bias_scr[h] = slope[h] * rel
    e_rel = (lax.broadcasted_iota(jnp.int32, (tk, n_sel), 1)
             - lax.broadcasted_iota(jnp.int32, (tk, n_sel), 0) // NSA_SEL_BLOCK)

    def flash_update(branches, off):
        qks = [[_dot(k_tile, qt_scr[h]) for h in range(nh)] for _, k_tile, _, _ in branches]
        ps, alphas = [], []
        for bi, (br, k_tile, valid, _) in enumerate(branches):
            for h in range(nh):
                r = br * nh + h
                shift = slope[h] * off
                s = qks[bi][h] - bias_scr[h, 0:k_tile.shape[0], :]
                if valid is not None:
                    s = jnp.where(valid[grp_of(h)], s, NEG)
                m_old = m_scr[r:r + 1, :]
                m_new = jnp.maximum(m_old, jnp.max(s, axis=0, keepdims=True) - shift)
                alphas.append(jnp.exp2(m_old - m_new))
                ps.append(jnp.exp2(s - (m_new + shift)).astype(BF16))
                m_scr[r:r + 1, :] = m_new
        for bi, (br, _, _, vts) in enumerate(branches):
            for h in range(nh):
                n = bi * nh + h
                acc_scr[br, h] = alphas[n] * acc_scr[br, h] + _dot(vts[grp_of(h)], ps[n])

    def tile(j, with_window):
        k0 = j * tk if isinstance(j, int) else pl.multiple_of(j * tk, tk)
        off = (q0 - k0).astype(F32)
        dist = rel + off
        causal = dist >= 0.0
        expand = (e_rel == k0 // NSA_SEL_BLOCK).astype(BF16)
        branches = [(0, k_ref[pl.ds(k0, tk), 0:LANES],
                     [(_dot(expand, sel[g]) > 0.5) & causal for g in range(NSA_KV)],
                     [vt_ref[j, g * VT_ROWS:(g + 1) * VT_ROWS, :] for g in range(NSA_KV)])]
        if with_window:
            valid_w = causal & (dist < float(NSA_WINDOW))
            branches.append((1, k_ref[pl.ds(k0, tk), LANES:2 * LANES], [valid_w] * NSA_KV,
                             [vt_ref[j, (NSA_KV + g) * VT_ROWS:(NSA_KV + g + 1) * VT_ROWS, :]
                              for g in range(NSA_KV)]))
        flash_update(branches, off)

    j_win = jnp.maximum(q0 - (NSA_WINDOW - 1), 0) // tk
    j_lo = jnp.minimum(j_sel, j_win)

    @pl.when((j_lo > 0) & jnp.logical_not(only_block0))
    def _():
        tile(0, False)

    @pl.when((j_lo > 0) & only_block0)
    def _():
        nb = NSA_SEL_BLOCK
        flash_update([(0, k_ref[0:nb, 0:LANES], None,
                       [vt_ref[0, g * VT_ROWS:(g + 1) * VT_ROWS, 0:nb] for g in range(NSA_KV)])], q0.astype(F32))

    def body(j, carry):
        @pl.when(j < j_win)
        def _():
            tile(j, False)

        @pl.when(j >= j_win)
        def _():
            tile(j, True)

        return carry

    lax.fori_loop(j_lo, (q0 + tq + tk - 1) // tk, body, 0)

    gst = jax.nn.sigmoid(gate_ref[...]).T
    heads = []
    for h in range(nh):
        g, hh = grp_of(h), h % NSA_GROUP
        gr = g * LANES + 3 * hh
        inv_s = 1.0 / acc_scr[0, h, HEAD_DIM:HEAD_DIM + 1, :]
        inv_w = 1.0 / acc_scr[1, h, HEAD_DIM:HEAD_DIM + 1, :]
        heads.append(gst[gr:gr + 1, :] * oc_scr[h, g * HEAD_DIM:(g + 1) * HEAD_DIM, :]
                     + (gst[gr + 1:gr + 2, :] * inv_s) * acc_scr[0, h, 0:HEAD_DIM, :]
                     + (gst[gr + 2:gr + 3, :] * inv_w) * acc_scr[1, h, 0:HEAD_DIM, :])
    o_ref[...] = jnp.concatenate(heads, axis=0).T.astype(BF16)


def _nsa(slopes, pbf, vt3, pf, kcmp, kcmpt, ovt, grp, batch, seq, tq, tk, k_col):
    nq = seq // tq
    ncp = kcmp.shape[2]
    n_sel = seq // NSA_SEL_BLOCK
    nkt = seq // tk
    qw = NSA_HEADS * LANES
    return pl.pallas_call(
        functools.partial(_nsa_kernel, tq=tq, tk=tk, seq=seq, ncp=ncp),
        grid=(batch, nq),
        in_specs=[pl.BlockSpec(memory_space=pltpu.SMEM),
                  pl.BlockSpec((tq, qw), lambda b, i: (b * nq + i, 0)),
                  pl.BlockSpec((seq, 2 * LANES), lambda b, i: (b, k_col // (2 * LANES))),
                  pl.BlockSpec((nkt, vt3.shape[1], tk), lambda b, i: (b, 0, 0)),
                  pl.BlockSpec((None, None, ncp, LANES), lambda b, i: (0, b, 0, 0)),
                  pl.BlockSpec((None, None, LANES, ncp), lambda b, i: (1, b, 0, 0)),
                  pl.BlockSpec((tq, NSA_KV * LANES), lambda b, i: (b * nq + i, 1)),
                  pl.BlockSpec((n_sel, ncp), lambda b, i: (0, 0)),
                  pl.BlockSpec((nkt, n_sel), lambda b, i: (0, 0))],
        out_specs=pl.BlockSpec((tq, NSA_HEADS * HEAD_DIM), lambda b, i: (b * nq + i, 0)),
        out_shape=jax.ShapeDtypeStruct((batch * seq, NSA_HEADS * HEAD_DIM), BF16),
        scratch_shapes=[pltpu.VMEM((NSA_HEADS, LANES, tq), BF16),
                        pltpu.VMEM((NSA_HEADS, LANES, tq), F32),
                        pltpu.VMEM((NSA_HEADS, tk, tq), F32),
                        pltpu.VMEM((2 * NSA_HEADS, tq), F32),
                        pltpu.VMEM((2, NSA_HEADS, VT_ROWS, tq), F32)],
        compiler_params=_cparams(("parallel", "arbitrary")),
        name="nsa",
    )(slopes, pbf, pbf, vt3, kcmp, kcmpt, pf, ovt, grp)


def _swa_kernel(slopes_ref, sinks_ref, q_ref, kv_ref, o_ref, *, tq):
    i = pl.program_id(1)
    q0 = i * tq
    kwid = tq + SWA_WINDOW
    start = pl.multiple_of(jnp.maximum(q0 - SWA_WINDOW, 0), SWA_WINDOW)
    kvw = kv_ref[pl.ds(start, kwid), :]
    dist = ((q0 - start) + lax.broadcasted_iota(jnp.int32, (tq, kwid), 0)
            - lax.broadcasted_iota(jnp.int32, (tq, kwid), 1)).astype(F32)
    valid = (dist >= 0.0) & (dist < float(SWA_WINDOW))
    lo_half = lax.broadcasted_iota(jnp.int32, (tq, LANES), 1) < HEAD_DIM
    qks = [_dot_nt(q_ref[:, h * LANES:(h + 1) * LANES], kvw) for h in range(SWA_HEADS)]
    es, dens = [], []
    for h in range(SWA_HEADS):
        s = jnp.where(valid, qks[h] - slopes_ref[h] * dist, NEG)
        sink = sinks_ref[h]
        m = jnp.maximum(jnp.max(s, axis=1, keepdims=True), sink)
        e = jnp.where(valid, jnp.exp(s - m), 0.0)
        dens.append(jnp.sum(e, axis=1, keepdims=True) + jnp.exp(sink - m))
        es.append(e.astype(BF16))
    outs = [_dot(es[h], kvw) / dens[h] for h in range(SWA_HEADS)]
    for pr in range(SWA_HEADS // 2):
        a_lo = pltpu.roll(outs[2 * pr], HEAD_DIM, 1)
        o_ref[:, pr * LANES:(pr + 1) * LANES] = jnp.where(lo_half, a_lo, outs[2 * pr + 1]).astype(BF16)


def _swa(slopes, sinks, pbf, batch, seq, tq, q_col, kv_col):
    nq = seq // tq
    qw = SWA_HEADS * LANES
    return pl.pallas_call(
        functools.partial(_swa_kernel, tq=tq),
        grid=(batch, nq),
        in_specs=[pl.BlockSpec(memory_space=pltpu.SMEM),
                  pl.BlockSpec(memory_space=pltpu.SMEM),
                  pl.BlockSpec((tq, qw), lambda b, i: (b * nq + i, q_col // qw)),
                  pl.BlockSpec((seq, LANES), lambda b, i: (b, kv_col // LANES))],
        out_specs=pl.BlockSpec((tq, SWA_HEADS * HEAD_DIM), lambda b, i: (b * nq + i, 0)),
        out_shape=jax.ShapeDtypeStruct((batch * seq, SWA_HEADS * HEAD_DIM), BF16),
        compiler_params=_cparams(("parallel", "parallel")),
        name="swa",
    )(slopes, sinks, pbf, pbf)


def _mid_kernel(x_ref, on_ref, os_ref, wo_ref, g2_ref, wq_ref, keys_ref,
                h_ref, h2_ref, st_ref):
    half = on_ref.shape[1]
    h = x_ref[...] + _dot(on_ref[...], wo_ref[:half, :]) + _dot(os_ref[...], wo_ref[half:, :])
    h_ref[...] = h
    ms = jnp.mean(h * h, axis=-1, keepdims=True)
    h2 = ((h * lax.rsqrt(ms + EPS)) * g2_ref[...]).astype(BF16)
    h2_ref[...] = h2
    qhs = [_dot(h2, wq_ref[:, 2 * hp * LANES:2 * (hp + 1) * LANES]).astype(BF16)
           for hp in range(PEER_HEADS)]
    for c in range(2 * PEER_HEADS):
        st_ref[c] = _dot_nt(keys_ref[c], qhs[c // 2][:, (c % 2) * LANES:(c % 2 + 1) * LANES])


def _mid(x2, o_n, o_s, wo, g2, wq, keys, tm):
    bt, d = x2.shape
    nk = 2 * PEER_HEADS
    return pl.pallas_call(
        _mid_kernel,
        grid=(bt // tm,),
        in_specs=[pl.BlockSpec((tm, d), lambda i: (i, 0)),
                  pl.BlockSpec((tm, o_n.shape[1]), lambda i: (i, 0)),
                  pl.BlockSpec((tm, o_s.shape[1]), lambda i: (i, 0)),
                  pl.BlockSpec(wo.shape, lambda i: (0, 0)),
                  pl.BlockSpec((1, d), lambda i: (0, 0)),
                  pl.BlockSpec(wq.shape, lambda i: (0, 0)),
                  pl.BlockSpec(keys.shape, lambda i: (0, 0, 0))],
        out_specs=[pl.BlockSpec((tm, d), lambda i: (i, 0)),
                   pl.BlockSpec((tm, d), lambda i: (i, 0)),
                   pl.BlockSpec((nk, PEER_NKEYS, tm), lambda i: (0, 0, i))],
        out_shape=[jax.ShapeDtypeStruct((bt, d), F32),
                   jax.ShapeDtypeStruct((bt, d), BF16),
                   jax.ShapeDtypeStruct((nk, PEER_NKEYS, bt), F32)],
        compiler_params=_cparams(("parallel",)),
        name="mid",
    )(x2, o_n, o_s, wo, g2, wq, keys)


def _pair_groups(k):
    groups = []
    i = 0
    while k // (i + 1) > 1:
        n = k // (i + 1)
        groups.append((i, n, -(-n // SUBLANES) * SUBLANES))
        i += 1
    return groups, i


def _peer_route_kernel(st_ref, u_ref, v_ref, w_ref, ub_ref, vb_ref,
                       v_scr, i_scr, cand_scr, c_scr, f_scr, sa_scr, sb_scr, sg_scr,
                       at_scr, bt_scr, gt_scr, s_scr, *, tt, pitch, group):
    k = PEER_TOPK
    nk = PEER_NKEYS

    @pl.when(pl.program_id(0) == 0)
    def _():
        at_scr[...] = jnp.zeros(at_scr.shape, F32)
        bt_scr[...] = jnp.zeros(bt_scr.shape, F32)
        gt_scr[...] = jnp.zeros(gt_scr.shape, F32)

    ub_ref[...] = u_ref[...].astype(BF16)
    vb_ref[...] = v_ref[...].astype(BF16)

    groups, tail = _pair_groups(k)
    rid128 = lax.broadcasted_iota(jnp.int32, (nk, tt), 0).astype(F32)
    codes, pads = [], []
    for (i, n, rows) in groups:
        j = lax.broadcasted_iota(jnp.int32, (rows, tt), 0)
        codes.append((j + i * k).astype(F32))
        pads.append(j < n)
    jt = lax.broadcasted_iota(jnp.int32, (k - tail, tt), 0)
    codes.append(((jt + tail) * k).astype(F32))
    code = jnp.concatenate(codes, axis=0)
    sub = lax.broadcasted_iota(jnp.int32, (nk, at_scr.shape[1]), 0).astype(F32)

    def build_group(tg):
        xs, ys = [], []
        for u in range(group):
            t = tg * group + u
            xs.append(jnp.where(at_scr[t:t + 1, :] == sub, gt_scr[t:t + 1, :], 0.0).astype(BF16))
            ys.append(jnp.where(bt_scr[t:t + 1, :] == sub, 1.0, 0.0).astype(BF16))
        ws = [_dot_nt(xs[u], ys[u]) for u in range(group)]
        for u in range(group):
            s_scr[pl.ds(tg * group + u, nk, stride=pitch), :] = ws[u]

    def convert_blocks(lo, hi):
        for i1 in range(lo, hi):
            w_ref[i1] = s_scr[i1 * pitch:i1 * pitch + tt, :].astype(BF16)

    def topk_head(h):
        for c in range(2):
            for r, (m, idx) in enumerate(_topk_axis0(st_ref[2 * h + c], k, rid128)):
                v_scr[c, r:r + 1, :] = m
                i_scr[c, r:r + 1, :] = idx
        row = 0
        for gi, (i, n, rows) in enumerate(groups):
            vals = v_scr[0, i:i + 1, :] + v_scr[1, 0:rows, :]
            cand_scr[row:row + rows, :] = jnp.where(pads[gi], vals, NEG)
            row += rows
        cand_scr[row:row + k - tail, :] = v_scr[0, tail:k, :] + v_scr[1, 0:1, :]
        for r, (m, f) in enumerate(_topk_axis0(cand_scr[...], k, code)):
            c_scr[r:r + 1, :] = m
            f_scr[r:r + 1, :] = f
        cs, fl = c_scr[...], f_scr[...]
        fi = jnp.floor(fl * (1.0 / k))
        fj = fl - fi * k
        a = jnp.zeros_like(fl)
        b = jnp.zeros_like(fl)
        for r in range(k):
            a = jnp.where(fi == float(r), i_scr[0, r:r + 1, :], a)
            b = jnp.where(fj == float(r), i_scr[1, r:r + 1, :], b)
        e = jnp.exp(cs - cs[0:1, :])
        sa_scr[h * k:(h + 1) * k, :] = a
        sb_scr[h * k:(h + 1) * k, :] = b
        sg_scr[h * k:(h + 1) * k, :] = 0.5 * (e / jnp.sum(e, axis=0, keepdims=True))
    n_groups = tt // group
    early = PEER_HEADS // 2
    h = 0
    for tg in range(n_groups):
        build_group(tg)
        if h < early and (tg + 1) * early >= (h + 1) * n_groups:
            topk_head(h)
            h += 1
    while h < early:
        topk_head(h)
        h += 1
    blk = nk // (PEER_HEADS - early)
    for q in range(PEER_HEADS - early):
        convert_blocks(q * blk, (q + 1) * blk)
        topk_head(early + q)
    at_scr[...] = sa_scr[...].T
    bt_scr[...] = sb_scr[...].T
    gt_scr[...] = sg_scr[...].T


def _peer_route(st, u, v, tt, group):
    nkk, _, bt = st.shape
    k = PEER_TOPK
    nk = PEER_NKEYS
    ns = PEER_HEADS * k
    nt = bt // tt
    ne, d = u.shape
    assert ne % nt == 0
    slab = pl.BlockSpec((ne // nt, d), lambda i: (jnp.minimum(i, nt - 1), 0))
    groups, tail = _pair_groups(k)
    n_cand = sum(rows for _, _, rows in groups) + k - tail
    pitch = tt + SUBLANES
    return pl.pallas_call(
        functools.partial(_peer_route_kernel, tt=tt, pitch=pitch, group=group),
        grid=(nt + 1,),
        in_specs=[pl.BlockSpec((nkk, nk, tt), lambda i: (0, 0, jnp.minimum(i, nt - 1))), slab, slab],
        out_specs=[pl.BlockSpec((nk, tt, nk), lambda i: (0, jnp.maximum(i - 1, 0), 0)), slab, slab],
        out_shape=[jax.ShapeDtypeStruct((nk, bt, nk), BF16),
                   jax.ShapeDtypeStruct((ne, d), BF16), jax.ShapeDtypeStruct((ne, d), BF16)],
        scratch_shapes=[pltpu.VMEM((2, k, tt), F32), pltpu.VMEM((2, k, tt), F32),
                        pltpu.VMEM((n_cand, tt), F32),
                        pltpu.VMEM((k, tt), F32), pltpu.VMEM((k, tt), F32),
                        pltpu.VMEM((ns, tt), F32), pltpu.VMEM((ns, tt), F32), pltpu.VMEM((ns, tt), F32),
                        pltpu.VMEM((tt, ns), F32), pltpu.VMEM((tt, ns), F32), pltpu.VMEM((tt, ns), F32),
                        pltpu.VMEM((nk * pitch, nk), F32)],
        compiler_params=_cparams(("arbitrary",)),
        name="peer_route",
    )(st, u, v)


def _peer_ffn_kernel(h2_ref, ue_ref, uo_ref, ve_ref, vo_ref, we_ref, wo_ref, h_ref, gf_ref, o_ref,
                     acc_scr, act0, act1):
    j = pl.program_id(1)
    last = pl.num_programs(1) - 1

    @pl.when(j == 0)
    def _():
        acc_scr[...] = jnp.zeros(acc_scr.shape, F32)
        act1[...] = jnp.zeros(act1.shape, F32)

    def consume(act_ref, w_ref, v_ref):
        z = []
        for c in range(w_ref.shape[0]):
            x = act_ref[:, c * LANES:(c + 1) * LANES]
            th = jnp.tanh(x * (GELU_C1 + GELU_C2 * (x * x)))
            z.append((x * th + x).astype(BF16) * w_ref[c])
        acc_scr[...] += _dot(jnp.concatenate(z, axis=1), v_ref[...])

    h2 = h2_ref[...]
    act0[...] = _dot_nt(h2, ue_ref[...])
    consume(act1, we_ref, ve_ref)

    @pl.when(j < last)
    def _():
        act1[...] = _dot_nt(h2, uo_ref[...])
        consume(act0, wo_ref, vo_ref)

    @pl.when(j == last)
    def _():
        h = h_ref[...] + acc_scr[...]
        ms = jnp.mean(h * h, axis=-1, keepdims=True)
        o_ref[...] = (h * lax.rsqrt(ms + EPS)) * gf_ref[...]


def _peer_ffn(h2, u, v, w, h, gf, tm, te):
    bt, d = h.shape
    ne = u.shape[0] // te
    assert ne % 2 == 0
    nb = te // LANES
    top = ne - 1
    produce_e = lambda i, j: (jnp.minimum(2 * j, top), 0)
    produce_o = lambda i, j: (jnp.minimum(2 * j + 1, top), 0)
    consume_e = lambda i, j: (jnp.maximum(2 * j - 1, 0), 0)
    consume_o = lambda i, j: (jnp.minimum(2 * j, top), 0)
    return pl.pallas_call(
        _peer_ffn_kernel,
        grid=(bt // tm, ne // 2 + 1),
        in_specs=[pl.BlockSpec((tm, d), lambda i, j: (i, 0)),
                  pl.BlockSpec((te, d), produce_e),
                  pl.BlockSpec((te, d), produce_o),
                  pl.BlockSpec((te, d), consume_e),
                  pl.BlockSpec((te, d), consume_o),
                  pl.BlockSpec((nb, tm, LANES), lambda i, j: (jnp.maximum(2 * j - 1, 0), i, 0)),
                  pl.BlockSpec((nb, tm, LANES), lambda i, j: (jnp.minimum(2 * j, top), i, 0)),
                  pl.BlockSpec((tm, d), lambda i, j: (i, 0)),
                  pl.BlockSpec((1, d), lambda i, j: (0, 0))],
        out_specs=pl.BlockSpec((tm, d), lambda i, j: (i, 0)),
        out_shape=jax.ShapeDtypeStruct((bt, d), F32),
        scratch_shapes=[pltpu.VMEM((tm, d), F32), pltpu.VMEM((tm, te), F32), pltpu.VMEM((tm, te), F32)],
        compiler_params=_cparams(("parallel", "arbitrary")),
        name="peer_ffn",
    )(h2, u, u, v, v, w, w, h, gf)


def _split_offsets():
    hd = HEAD_DIM
    sizes = [NSA_HEADS * hd, NSA_KV * hd, NSA_KV * hd, NSA_KV * hd, NSA_KV * hd, NSA_KV * hd,
             NSA_KV * hd, NSA_HEADS * 3, SWA_HEADS * hd, hd, hd]
    return [0] + [int(c) for c in np.cumsum(sizes)]


def _pick(n, prefs):
    for p in prefs:
        if n % p == 0:
            return p
    return n


def kernel(x, ln1_g, w_in, cmp_pe_k, cmp_w1_k, cmp_w2_k, cmp_pe_v, cmp_w1_v, cmp_w2_v, swa_sinks, w_out, ln2_g, peer_wq, peer_keys, peer_u, peer_v, lnf_g):
    batch, seq, d = x.shape
    bt = batch * seq
    hd = HEAD_DIM
    nsa_tq, nsa_tk = 256, 256
    assert ln1_g.shape[0] == 1, "single layer"
    assert seq % 512 == 0
    slopes = jnp.asarray((2.0 ** (-8.0 * (np.arange(N_HEADS) + 1) / N_HEADS)).astype(np.float32))

    off = _split_offsets()
    w = w_in[0]
    col = lambda k: w[:, off[k]:off[k + 1]]
    scale = hd ** -0.5
    z64 = jnp.zeros((d, hd), F32)
    qn, qs = col(0) * (scale * LOG2E), col(8) * scale
    qn_exp = []
    for h in range(NSA_HEADS):
        qh = qn[:, h * hd:(h + 1) * hd]
        qn_exp += [qh, z64] if h // NSA_GROUP == 0 else [z64, qh]
    qs_exp = []
    for h in range(SWA_HEADS):
        qs_exp += [qs[:, h * hd:(h + 1) * hd], z64]
    gt = col(7)
    gpad = jnp.zeros((d, LANES - NSA_GROUP * 3), F32)
    gcols = []
    for g in range(NSA_KV):
        gcols += [gt[:, g * NSA_GROUP * 3:(g + 1) * NSA_GROUP * 3], gpad]
    w_all = jnp.concatenate(qn_exp + qs_exp + [col(3), col(5), col(9), col(10)]
                            + [col(4), col(6)] + [col(1), col(2)] + gcols, axis=1).astype(BF16)
    swa_q_col = NSA_HEADS * LANES
    nsa_k_col = swa_q_col + SWA_HEADS * LANES
    swa_kv_col = nsa_k_col + 2 * LANES
    n_bf = swa_kv_col + LANES
    n_vt = 2 * LANES

    x2 = x.reshape(bt, d)
    pbf, vt3, pf = _proj(x2, ln1_g[0][None, :], w_all, n_bf, n_vt, 512, nsa_tk)

    ncp = seq // NSA_CMP_STRIDE
    pe_dup = jnp.stack([jnp.concatenate([pe, pe], axis=1) for pe in (cmp_pe_k[0], cmp_pe_v[0])])
    w1_dup = jnp.stack([jnp.concatenate([w1.reshape(NSA_CMP_LEN, hd, -1)] * 2, axis=1)
                        .reshape(2, NSA_CMP_STRIDE * LANES, -1)
                        for w1 in (cmp_w1_k[0], cmp_w1_v[0])]).astype(BF16)
    zc = jnp.zeros((NSA_CMP_HIDDEN, hd), F32)
    w2e = jnp.stack([jnp.stack([jnp.concatenate([w2, zc], axis=1), jnp.concatenate([zc, w2], axis=1)])
                     for w2 in (cmp_w2_k[0], cmp_w2_v[0])]).astype(BF16)
    kcmp, kcmpt = _compress(pf, pe_dup, w1_dup, w2e, batch, seq)

    n_sel = seq // NSA_SEL_BLOCK
    c0 = np.arange(ncp)[None, :] * NSA_CMP_STRIDE
    s0 = np.arange(n_sel)[:, None] * NSA_SEL_BLOCK
    ovt = ((c0 < s0 + NSA_SEL_BLOCK) & (c0 + NSA_CMP_LEN > s0) & (np.arange(ncp)[None, :] < ncp - 1))
    ovt = jnp.asarray(ovt.astype(np.float32), BF16)
    grp = (np.arange(n_sel)[None, :] * NSA_SEL_BLOCK // nsa_tk) == np.arange(seq // nsa_tk)[:, None]
    grp = jnp.asarray(grp.astype(np.float32), BF16)

    o_n = _nsa(slopes, pbf, vt3, pf, kcmp, kcmpt, ovt, grp, batch, seq, nsa_tq, nsa_tk, nsa_k_col)
    o_s = _swa(slopes, swa_sinks[0], pbf, batch, seq, 128, swa_q_col, swa_kv_col)

    keys = peer_keys[0].reshape(2 * PEER_HEADS, PEER_NKEYS, -1).astype(BF16)
    h, h2, st = _mid(x2, o_n, o_s, w_out[0].astype(BF16), ln2_g[0][None, :],
                     peer_wq[0].astype(BF16), keys, _pick(bt, (512, 256, 128)))
    wd, ub, vb = _peer_route(st, peer_u[0], peer_v[0], 256, 32)
    out = _peer_ffn(h2, ub, vb, wd, h, lnf_g[None, :], _pick(bt, (1024, 512, 256, 128)), 512)
    return out.reshape(batch, seq, d)
```

```python
import functools

import numpy as np
import jax
import jax.numpy as jnp
from jax import lax
from jax.experimental import pallas as pl
from jax.experimental.pallas import tpu as pltpu

F32 = jnp.float32
BF16 = jnp.bfloat16

HEAD_DIM = 64
N_HEADS = 16
NSA_HEADS = 8
NSA_KV = 2
NSA_GROUP = 4
SWA_HEADS = 8
NSA_CMP_LEN = 32
NSA_CMP_STRIDE = 16
NSA_CMP_HIDDEN = 256
NSA_SEL_BLOCK = 64
NSA_TOPN = 16
NSA_WINDOW = 512
SWA_WINDOW = 128
FORCE_SCORE = 1.0e4
PEER_HEADS = 8
PEER_NKEYS = 128
PEER_TOPK = 16
EPS = 1e-6
NEG = -1.0e30
LANES = 128
SUBLANES = 8
BF16_ROWS = 16
VT_ROWS = HEAD_DIM + BF16_ROWS
LOG2E = 1.4426950408889634
GELU_C1 = 0.7978845608028654
GELU_C2 = 0.044715 * GELU_C1
VMEM_LIMIT = 56 * 1024 * 1024

PROJ_TM = 512
NSA_TQ = 256
NSA_TK = 256
SWA_TQ = 128
MID_TM = 512
ROUTE_TT = 256
ROUTE_GROUP = 32
FFN_TM = 1024
FFN_TE = 512


def _dot(a, b):
    return jnp.dot(a, b, preferred_element_type=F32)


def _dot_nt(a, b):
    return lax.dot_general(a, b, (((1,), (1,)), ((), ())), preferred_element_type=F32)


def _cparams(sem):
    return pltpu.CompilerParams(dimension_semantics=sem, vmem_limit_bytes=VMEM_LIMIT)


def _topk_axis0(v, k, code):
    big = 3.0e38
    out = []
    for _ in range(k):
        m = jnp.max(v, axis=0, keepdims=True)
        c = jnp.min(jnp.where(v == m, code, big), axis=0, keepdims=True)
        out.append((m, c))
        v = jnp.where(code == c, NEG, v)
    return out


def _proj_kernel(x_ref, g_ref, w_ref, obf_ref, vt_ref, of_ref, *, n_bf, n_vt, tk):
    x = x_ref[...]
    ms = jnp.mean(x * x, axis=-1, keepdims=True)
    a = ((x * lax.rsqrt(ms + EPS)) * g_ref[...]).astype(BF16)
    for c0 in range(0, n_bf, 4 * LANES):
        c1 = min(c0 + 4 * LANES, n_bf)
        obf_ref[:, c0:c1] = _dot(a, w_ref[:, c0:c1]).astype(BF16)
    yvt = _dot(a, w_ref[:, n_bf:n_bf + n_vt]).T
    ones = jnp.ones((VT_ROWS - HEAD_DIM, tk), BF16)
    for s in range(vt_ref.shape[0]):
        for c in range(n_vt // HEAD_DIM):
            vt_ref[s, c * VT_ROWS:c * VT_ROWS + HEAD_DIM, :] = (
                yvt[c * HEAD_DIM:(c + 1) * HEAD_DIM, s * tk:(s + 1) * tk].astype(BF16))
            vt_ref[s, c * VT_ROWS + HEAD_DIM:(c + 1) * VT_ROWS, :] = ones
    of_ref[...] = _dot(a, w_ref[:, n_bf + n_vt:])


def _proj(x2, g, w_all, n_bf, n_vt, tm, tk):
    bt, d = x2.shape
    n_all = w_all.shape[1]
    n_f = n_all - n_bf - n_vt
    vt_rows = n_vt // HEAD_DIM * VT_ROWS
    return pl.pallas_call(
        functools.partial(_proj_kernel, n_bf=n_bf, n_vt=n_vt, tk=tk),
        grid=(bt // tm,),
        in_specs=[pl.BlockSpec((tm, d), lambda i: (i, 0)),
                  pl.BlockSpec((1, d), lambda i: (0, 0)),
                  pl.BlockSpec((d, n_all), lambda i: (0, 0))],
        out_specs=[pl.BlockSpec((tm, n_bf), lambda i: (i, 0)),
                   pl.BlockSpec((tm // tk, vt_rows, tk), lambda i: (i, 0, 0)),
                   pl.BlockSpec((tm, n_f), lambda i: (i, 0))],
        out_shape=[jax.ShapeDtypeStruct((bt, n_bf), BF16),
                   jax.ShapeDtypeStruct((bt // tk, vt_rows, tk), BF16),
                   jax.ShapeDtypeStruct((bt, n_f), F32)],
        compiler_params=_cparams(("parallel",)),
        name="proj",
    )(x2, g, w_all)


def _compress_kernel(c_ref, pe_ref, w1_ref, w2_ref, o_ref, ot_ref, *, ncp):
    lo_half = lax.broadcasted_iota(jnp.int32, (ncp, LANES), 1) < HEAD_DIM
    xa = [[], []]
    xb = [[], []]
    for l in range(NSA_CMP_STRIDE):
        x = c_ref[pl.ds(l, ncp, stride=NSA_CMP_STRIDE), :]
        a = x + pe_ref[l:l + 1, :]
        b = x + pe_ref[NSA_CMP_STRIDE + l:NSA_CMP_STRIDE + l + 1, :]
        xa[0].append(jnp.where(lo_half, a, 0.0).astype(BF16))
        xa[1].append(jnp.where(lo_half, 0.0, a).astype(BF16))
        xb[0].append(jnp.where(lo_half, b, 0.0).astype(BF16))
        xb[1].append(jnp.where(lo_half, 0.0, b).astype(BF16))
    acc = None
    for g in range(NSA_KV):
        ya = _dot(jnp.concatenate(xa[g], axis=1), w1_ref[0])
        yb = _dot(jnp.concatenate(xb[g], axis=1), w1_ref[1])
        hid = ya + pltpu.roll(yb, ncp - 1, 0)
        act = jax.nn.gelu(hid).astype(BF16)
        t = _dot(act, w2_ref[g])
        acc = t if acc is None else acc + t
    o_ref[...] = acc.astype(BF16)
    ot_ref[...] = acc.T.astype(BF16)


def _compress(pf, pe_dup, w1_dup, w2e, batch, seq):
    ncp = seq // NSA_CMP_STRIDE
    kdim = NSA_CMP_STRIDE * LANES
    return pl.pallas_call(
        functools.partial(_compress_kernel, ncp=ncp),
        grid=(2, batch),
        in_specs=[pl.BlockSpec((seq, LANES), lambda s, i: (i, s)),
                  pl.BlockSpec((None, NSA_CMP_LEN, LANES), lambda s, i: (s, 0, 0)),
                  pl.BlockSpec((None, 2, kdim, NSA_CMP_HIDDEN), lambda s, i: (s, 0, 0, 0)),
                  pl.BlockSpec((None, NSA_KV, NSA_CMP_HIDDEN, LANES), lambda s, i: (s, 0, 0, 0))],
        out_specs=[pl.BlockSpec((None, None, ncp, LANES), lambda s, i: (s, i, 0, 0)),
                   pl.BlockSpec((None, None, LANES, ncp), lambda s, i: (s, i, 0, 0))],
        out_shape=[jax.ShapeDtypeStruct((2, batch, ncp, LANES), BF16),
                   jax.ShapeDtypeStruct((2, batch, LANES, ncp), BF16)],
        compiler_params=_cparams(("parallel", "parallel")),
        name="compress",
    )(pf, pe_dup, w1_dup, w2e)


def _nsa_kernel(slopes_ref, q_ref, k_ref, vt_ref, kc_ref, vct_ref, gate_ref, ovt_ref, grp_ref, o_ref,
                qt_scr, oc_scr, bias_scr, m_scr, acc_scr, *, tq, tk, seq, ncp):
    i = pl.program_id(1)
    q0 = i * tq
    n_sel = seq // NSA_SEL_BLOCK
    n_cmp = ncp - 1
    nh = NSA_HEADS
    grp_of = lambda h: h // NSA_GROUP
    slope = [slopes_ref[SWA_HEADS + h] * LOG2E for h in range(nh)]

    kc = kc_ref[...]
    vct = vct_ref[...]
    n_c = lax.broadcasted_iota(jnp.int32, (ncp, tq), 0)
    t_c = q0 + lax.broadcasted_iota(jnp.int32, (ncp, tq), 1)
    dist_c = (t_c - (n_c * NSA_CMP_STRIDE + (NSA_CMP_LEN - 1))).astype(F32)
    valid_c = (dist_c >= 0.0) & (n_c < n_cmp)
    qts = []
    for h in range(nh):
        qts.append(q_ref[:, h * LANES:(h + 1) * LANES].astype(F32).T.astype(BF16))
        qt_scr[h] = qts[h]
    scs = [_dot(kc, qt) for qt in qts]
    psum = [jnp.zeros((ncp, tq), F32) for _ in range(NSA_KV)]
    pcs = []
    for h in range(nh):
        s = jnp.where(valid_c, scs[h] - slope[h] * dist_c, NEG)
        m = jnp.max(s, axis=0, keepdims=True)
        e = jnp.where(valid_c, jnp.exp2(s - m), 0.0)
        den = jnp.maximum(jnp.sum(e, axis=0, keepdims=True), 1e-30)
        p = e * (1.0 / den)
        psum[grp_of(h)] = psum[grp_of(h)] + p
        pcs.append(p.astype(BF16))
    for h in range(nh):
        oc_scr[h] = _dot(vct, pcs[h])

    jb = lax.broadcasted_iota(jnp.int32, (n_sel, tq), 0)
    blk_t = (q0 + lax.broadcasted_iota(jnp.int32, (n_sel, tq), 1)) // NSA_SEL_BLOCK
    forced = ((jb == 0) | (jb == blk_t) | (jb == blk_t - 1)) & (jb <= blk_t)
    jbf = jb.astype(F32)
    nkt = grp_ref.shape[0]
    tid = lax.broadcasted_iota(jnp.int32, (nkt, 1), 0)
    sel, j_sel, only_block0 = [], None, None
    for g in range(NSA_KV):
        imp = _dot(ovt_ref[...], psum[g].astype(BF16))
        imp = jnp.where(forced | (jb > blk_t), NEG, imp)
        sel_t = jnp.where(forced, 1.0, 0.0)
        for m_r, idx_r in _topk_axis0(imp, min(NSA_TOPN, n_sel) - 3, jbf):
            sel_t = jnp.where((jbf == idx_r) & (m_r > 0.5 * NEG), 1.0, sel_t)
        sel.append(sel_t.astype(BF16))
        cnt = jnp.max(_dot(grp_ref[...], sel[g]), axis=1, keepdims=True)
        j_g = jnp.min(jnp.where((cnt > 0.5) & (tid > 0), tid.astype(F32), float(nkt))).astype(jnp.int32)
        b0_g = jnp.max(cnt[0:1, :]) < 1.5
        j_sel = j_g if j_sel is None else jnp.minimum(j_sel, j_g)
        only_block0 = b0_g if only_block0 is None else (only_block0 & b0_g)

    m_scr[...] = jnp.full(m_scr.shape, NEG, F32)
    acc_scr[...] = jnp.zeros(acc_scr.shape, F32)
    rel = (lax.broadcasted_iota(jnp.int32, (tk, tq), 1)
           - lax.broadcasted_iota(jnp.int32, (tk, tq), 0)).astype(F32)
    for h in range(nh):
        bias_scr[h] = slope[h] * rel
    e_rel = (lax.broadcasted_iota(jnp.int32, (tk, n_sel), 1)
             - lax.broadcasted_iota(jnp.int32, (tk, n_sel), 0) // NSA_SEL_BLOCK)

    def flash_update(branches, off):
        qks = [[_dot(k_tile, qt_scr[h]) for h in range(nh)] for _, k_tile, _, _ in branches]
        ps, alphas = [], []
        for bi, (br, k_tile, valid, _) in enumerate(branches):
            for h in range(nh):
                r = br * nh + h
                shift = slope[h] * off
                s = qks[bi][h] - bias_scr[h, 0:k_tile.shape[0], :]
                if valid is not None:
                    s = jnp.where(valid[grp_of(h)], s, NEG)
                m_old = m_scr[r:r + 1, :]
                m_new = jnp.maximum(m_old, jnp.max(s, axis=0, keepdims=True) - shift)
                alphas.append(jnp.exp2(m_old - m_new))
                ps.append(jnp.exp2(s - (m_new + shift)).astype(BF16))
                m_scr[r:r + 1, :] = m_new
        for bi, (br, _, _, vts) in enumerate(branches):
            for h in range(nh):
                n = bi * nh + h
                acc_scr[br, h] = alphas[n] * acc_scr[br, h] + _dot(vts[grp_of(h)], ps[n])

    def tile(j, with_window):
        k0 = j * tk if isinstance(j, int) else pl.multiple_of(j * tk, tk)
        off = (q0 - k0).astype(F32)
        dist = rel + off
        causal = dist >= 0.0
        expand = (e_rel == k0 // NSA_SEL_BLOCK).astype(BF16)
        branches = [(0, k_ref[pl.ds(k0, tk), 0:LANES],
                     [(_dot(expand, sel[g]) > 0.5) & causal for g in range(NSA_KV)],
                     [vt_ref[j, g * VT_ROWS:(g + 1) * VT_ROWS, :] for g in range(NSA_KV)])]
        if with_window:
            valid_w = causal & (dist < float(NSA_WINDOW))
            branches.append((1, k_ref[pl.ds(k0, tk), LANES:2 * LANES], [valid_w] * NSA_KV,
                             [vt_ref[j, (NSA_KV + g) * VT_ROWS:(NSA_KV + g + 1) * VT_ROWS, :]
                              for g in range(NSA_KV)]))
        flash_update(branches, off)

    j_win = jnp.maximum(q0 - (NSA_WINDOW - 1), 0) // tk
    j_lo = jnp.minimum(j_sel, j_win)

    @pl.when((j_lo > 0) & jnp.logical_not(only_block0))
    def _():
        tile(0, False)

    @pl.when((j_lo > 0) & only_block0)
    def _():
        nb = NSA_SEL_BLOCK
        flash_update([(0, k_ref[0:nb, 0:LANES], None,
                       [vt_ref[0, g * VT_ROWS:(g + 1) * VT_ROWS, 0:nb] for g in range(NSA_KV)])], q0.astype(F32))

    def body(j, carry):
        @pl.when(j < j_win)
        def _():
            tile(j, False)

        @pl.when(j >= j_win)
        def _():
            tile(j, True)

        return carry

    lax.fori_loop(j_lo, (q0 + tq + tk - 1) // tk, body, 0)

    gst = jax.nn.sigmoid(gate_ref[...]).T
    heads = []
    for h in range(nh):
        g, hh = grp_of(h), h % NSA_GROUP
        gr = g * LANES + 3 * hh
        inv_s = 1.0 / acc_scr[0, h, HEAD_DIM:HEAD_DIM + 1, :]
        inv_w = 1.0 / acc_scr[1, h, HEAD_DIM:HEAD_DIM + 1, :]
        heads.append(gst[gr:gr + 1, :] * oc_scr[h, g * HEAD_DIM:(g + 1) * HEAD_DIM, :]
                     + (gst[gr + 1:gr + 2, :] * inv_s) * acc_scr[0, h, 0:HEAD_DIM, :]
                     + (gst[gr + 2:gr + 3, :] * inv_w) * acc_scr[1, h, 0:HEAD_DIM, :])
    o_ref[...] = jnp.concatenate(heads, axis=0).T.astype(BF16)


def _nsa(slopes, pbf, vt3, pf, kcmp, kcmpt, ovt, grp, batch, seq, tq, tk, k_col):
    nq = seq // tq
    ncp = kcmp.shape[2]
    n_sel = seq // NSA_SEL_BLOCK
    nkt = seq // tk
    qw = NSA_HEADS * LANES
    return pl.pallas_call(
        functools.partial(_nsa_kernel, tq=tq, tk=tk, seq=seq, ncp=ncp),
        grid=(batch, nq),
        in_specs=[pl.BlockSpec(memory_space=pltpu.SMEM),
                  pl.BlockSpec((tq, qw), lambda b, i: (b * nq + i, 0)),
                  pl.BlockSpec((seq, 2 * LANES), lambda b, i: (b, k_col // (2 * LANES))),
                  pl.BlockSpec((nkt, vt3.shape[1], tk), lambda b, i: (b, 0, 0)),
                  pl.BlockSpec((None, None, ncp, LANES), lambda b, i: (0, b, 0, 0)),
                  pl.BlockSpec((None, None, LANES, ncp), lambda b, i: (1, b, 0, 0)),
                  pl.BlockSpec((tq, NSA_KV * LANES), lambda b, i: (b * nq + i, 1)),
                  pl.BlockSpec((n_sel, ncp), lambda b, i: (0, 0)),
                  pl.BlockSpec((nkt, n_sel), lambda b, i: (0, 0))],
        out_specs=pl.BlockSpec((tq, NSA_HEADS * HEAD_DIM), lambda b, i: (b * nq + i, 0)),
        out_shape=jax.ShapeDtypeStruct((batch * seq, NSA_HEADS * HEAD_DIM), BF16),
        scratch_shapes=[pltpu.VMEM((NSA_HEADS, LANES, tq), BF16),
                        pltpu.VMEM((NSA_HEADS, LANES, tq), F32),
                        pltpu.VMEM((NSA_HEADS, tk, tq), F32),
                        pltpu.VMEM((2 * NSA_HEADS, tq), F32),
                        pltpu.VMEM((2, NSA_HEADS, VT_ROWS, tq), F32)],
        compiler_params=_cparams(("parallel", "arbitrary")),
        name="nsa",
    )(slopes, pbf, pbf, vt3, kcmp, kcmpt, pf, ovt, grp)


def _swa_kernel(slopes_ref, sinks_ref, q_ref, kv_ref, o_ref, *, tq):
    i = pl.program_id(1)
    q0 = i * tq
    kwid = tq + SWA_WINDOW
    start = pl.multiple_of(jnp.maximum(q0 - SWA_WINDOW, 0), SWA_WINDOW)
    kvw = kv_ref[pl.ds(start, kwid), :]
    dist = ((q0 - start) + lax.broadcasted_iota(jnp.int32, (tq, kwid), 0)
            - lax.broadcasted_iota(jnp.int32, (tq, kwid), 1)).astype(F32)
    valid = (dist >= 0.0) & (dist < float(SWA_WINDOW))
    lo_half = lax.broadcasted_iota(jnp.int32, (tq, LANES), 1) < HEAD_DIM
    qks = [_dot_nt(q_ref[:, h * LANES:(h + 1) * LANES], kvw) for h in range(SWA_HEADS)]
    es, dens = [], []
    for h in range(SWA_HEADS):
        s = jnp.where(valid, qks[h] - slopes_ref[h] * dist, NEG)
        sink = sinks_ref[h]
        m = jnp.maximum(jnp.max(s, axis=1, keepdims=True), sink)
        e = jnp.where(valid, jnp.exp(s - m), 0.0)
        dens.append(jnp.sum(e, axis=1, keepdims=True) + jnp.exp(sink - m))
        es.append(e.astype(BF16))
    outs = [_dot(es[h], kvw) / dens[h] for h in range(SWA_HEADS)]
    for pr in range(SWA_HEADS // 2):
        a_lo = pltpu.roll(outs[2 * pr], HEAD_DIM, 1)
        o_ref[:, pr * LANES:(pr + 1) * LANES] = jnp.where(lo_half, a_lo, outs[2 * pr + 1]).astype(BF16)


def _swa(slopes, sinks, pbf, batch, seq, tq, q_col, kv_col):
    nq = seq // tq
    qw = SWA_HEADS * LANES
    return pl.pallas_call(
        functools.partial(_swa_kernel, tq=tq),
        grid=(batch, nq),
        in_specs=[pl.BlockSpec(memory_space=pltpu.SMEM),
                  pl.BlockSpec(memory_space=pltpu.SMEM),
                  pl.BlockSpec((tq, qw), lambda b, i: (b * nq + i, q_col // qw)),
                  pl.BlockSpec((seq, LANES), lambda b, i: (b, kv_col // LANES))],
        out_specs=pl.BlockSpec((tq, SWA_HEADS * HEAD_DIM), lambda b, i: (b * nq + i, 0)),
        out_shape=jax.ShapeDtypeStruct((batch * seq, SWA_HEADS * HEAD_DIM), BF16),
        compiler_params=_cparams(("parallel", "parallel")),
        name="swa",
    )(slopes, sinks, pbf, pbf)


def _mid_kernel(x_ref, on_ref, os_ref, wo_ref, g2_ref, wq_ref, keys_ref,
                h_ref, h2_ref, st_ref):
    half = on_ref.shape[1]
    h = x_ref[...] + _dot(on_ref[...], wo_ref[:half, :]) + _dot(os_ref[...], wo_ref[half:, :])
    h_ref[...] = h
    ms = jnp.mean(h * h, axis=-1, keepdims=True)
    h2 = ((h * lax.rsqrt(ms + EPS)) * g2_ref[...]).astype(BF16)
    h2_ref[...] = h2
    qhs = [_dot(h2, wq_ref[:, 2 * hp * LANES:2 * (hp + 1) * LANES]).astype(BF16)
           for hp in range(PEER_HEADS)]
    for c in range(2 * PEER_HEADS):
        st_ref[c] = _dot_nt(keys_ref[c], qhs[c // 2][:, (c % 2) * LANES:(c % 2 + 1) * LANES])


def _mid(x2, o_n, o_s, wo, g2, wq, keys, tm):
    bt, d = x2.shape
    nk = 2 * PEER_HEADS
    return pl.pallas_call(
        _mid_kernel,
        grid=(bt // tm,),
        in_specs=[pl.BlockSpec((tm, d), lambda i: (i, 0)),
                  pl.BlockSpec((tm, o_n.shape[1]), lambda i: (i, 0)),
                  pl.BlockSpec((tm, o_s.shape[1]), lambda i: (i, 0)),
                  pl.BlockSpec(wo.shape, lambda i: (0, 0)),
                  pl.BlockSpec((1, d), lambda i: (0, 0)),
                  pl.BlockSpec(wq.shape, lambda i: (0, 0)),
                  pl.BlockSpec(keys.shape, lambda i: (0, 0, 0))],
        out_specs=[pl.BlockSpec((tm, d), lambda i: (i, 0)),
                   pl.BlockSpec((tm, d), lambda i: (i, 0)),
                   pl.BlockSpec((nk, PEER_NKEYS, tm), lambda i: (0, 0, i))],
        out_shape=[jax.ShapeDtypeStruct((bt, d), F32),
                   jax.ShapeDtypeStruct((bt, d), BF16),
                   jax.ShapeDtypeStruct((nk, PEER_NKEYS, bt), F32)],
        compiler_params=_cparams(("parallel",)),
        name="mid",
    )(x2, o_n, o_s, wo, g2, wq, keys)


def _pair_groups(k):
    groups = []
    i = 0
    while k // (i + 1) > 1:
        n = k // (i + 1)
        groups.append((i, n, -(-n // SUBLANES) * SUBLANES))
        i += 1
    return groups, i


def _peer_route_kernel(st_ref, u_ref, v_ref, w_ref, ub_ref, vb_ref,
                       v_scr, i_scr, cand_scr, c_scr, f_scr, sa_scr, sb_scr, sg_scr,
                       at_scr, bt_scr, gt_scr, s_scr, *, tt, pitch, group):
    k = PEER_TOPK
    nk = PEER_NKEYS

    @pl.when(pl.program_id(0) == 0)
    def _():
        at_scr[...] = jnp.zeros(at_scr.shape, F32)
        bt_scr[...] = jnp.zeros(bt_scr.shape, F32)
        gt_scr[...] = jnp.zeros(gt_scr.shape, F32)

    ub_ref[...] = u_ref[...].astype(BF16)
    vb_ref[...] = v_ref[...].astype(BF16)

    groups, tail = _pair_groups(k)
    rid128 = lax.broadcasted_iota(jnp.int32, (nk, tt), 0).astype(F32)
    codes, pads = [], []
    for (i, n, rows) in groups:
        j = lax.broadcasted_iota(jnp.int32, (rows, tt), 0)
        codes.append((j + i * k).astype(F32))
        pads.append(j < n)
    jt = lax.broadcasted_iota(jnp.int32, (k - tail, tt), 0)
    codes.append(((jt + tail) * k).astype(F32))
    code = jnp.concatenate(codes, axis=0)
    sub = lax.broadcasted_iota(jnp.int32, (nk, at_scr.shape[1]), 0).astype(F32)

    def build_group(tg):
        xs, ys = [], []
        for u in range(group):
            t = tg * group + u
            xs.append(jnp.where(at_scr[t:t + 1, :] == sub, gt_scr[t:t + 1, :], 0.0).astype(BF16))
            ys.append(jnp.where(bt_scr[t:t + 1, :] == sub, 1.0, 0.0).astype(BF16))
        ws = [_dot_nt(xs[u], ys[u]) for u in range(group)]
        for u in range(group):
            s_scr[pl.ds(tg * group + u, nk, stride=pitch), :] = ws[u]

    def convert_blocks(lo, hi):
        for i1 in range(lo, hi):
            w_ref[i1] = s_scr[i1 * pitch:i1 * pitch + tt, :].astype(BF16)

    def topk_head(h):
        for c in range(2):
            for r, (m, idx) in enumerate(_topk_axis0(st_ref[2 * h + c], k, rid128)):
                v_scr[c, r:r + 1, :] = m
                i_scr[c, r:r + 1, :] = idx
        row = 0
        for gi, (i, n, rows) in enumerate(groups):
            vals = v_scr[0, i:i + 1, :] + v_scr[1, 0:rows, :]
            cand_scr[row:row + rows, :] = jnp.where(pads[gi], vals, NEG)
            row += rows
        cand_scr[row:row + k - tail, :] = v_scr[0, tail:k, :] + v_scr[1, 0:1, :]
        for r, (m, f) in enumerate(_topk_axis0(cand_scr[...], k, code)):
            c_scr[r:r + 1, :] = m
            f_scr[r:r + 1, :] = f
        cs, fl = c_scr[...], f_scr[...]
        fi = jnp.floor(fl * (1.0 / k))
        fj = fl - fi * k
        a = jnp.zeros_like(fl)
        b = jnp.zeros_like(fl)
        for r in range(k):
            a = jnp.where(fi == float(r), i_scr[0, r:r + 1, :], a)
            b = jnp.where(fj == float(r), i_scr[1, r:r + 1, :], b)
        e = jnp.exp(cs - cs[0:1, :])
        sa_scr[h * k:(h + 1) * k, :] = a
        sb_scr[h * k:(h + 1) * k, :] = b
        sg_scr[h * k:(h + 1) * k, :] = 0.5 * (e / jnp.sum(e, axis=0, keepdims=True))
    n_groups = tt // group
    early = PEER_HEADS // 2
    h = 0
    for tg in range(n_groups):
        build_group(tg)
        if h < early and (tg + 1) * early >= (h + 1) * n_groups:
            topk_head(h)
            h += 1
    while h < early:
        topk_head(h)
        h += 1
    blk = nk // (PEER_HEADS - early)
    for q in range(PEER_HEADS - early):
        convert_blocks(q * blk, (q + 1) * blk)
        topk_head(early + q)
    at_scr[...] = sa_scr[...].T
    bt_scr[...] = sb_scr[...].T
    gt_scr[...] = sg_scr[...].T


def _peer_route(st, u, v, tt, group):
    nkk, _, bt = st.shape
    k = PEER_TOPK
    nk = PEER_NKEYS
    ns = PEER_HEADS * k
    nt = bt // tt
    ne, d = u.shape
    assert ne % nt == 0
    slab = pl.BlockSpec((ne // nt, d), lambda i: (jnp.minimum(i, nt - 1), 0))
    groups, tail = _pair_groups(k)
    n_cand = sum(rows for _, _, rows in groups) + k - tail
    pitch = tt + SUBLANES
    return pl.pallas_call(
        functools.partial(_peer_route_kernel, tt=tt, pitch=pitch, group=group),
        grid=(nt + 1,),
        in_specs=[pl.BlockSpec((nkk, nk, tt), lambda i: (0, 0, jnp.minimum(i, nt - 1))), slab, slab],
        out_specs=[pl.BlockSpec((nk, tt, nk), lambda i: (0, jnp.maximum(i - 1, 0), 0)), slab, slab],
        out_shape=[jax.ShapeDtypeStruct((nk, bt, nk), BF16),
                   jax.ShapeDtypeStruct((ne, d), BF16), jax.ShapeDtypeStruct((ne, d), BF16)],
        scratch_shapes=[pltpu.VMEM((2, k, tt), F32), pltpu.VMEM((2, k, tt), F32),
                        pltpu.VMEM((n_cand, tt), F32),
                        pltpu.VMEM((k, tt), F32), pltpu.VMEM((k, tt), F32),
                        pltpu.VMEM((ns, tt), F32), pltpu.VMEM((ns, tt), F32), pltpu.VMEM((ns, tt), F32),
                        pltpu.VMEM((tt, ns), F32), pltpu.VMEM((tt, ns), F32), pltpu.VMEM((tt, ns), F32),
                        pltpu.VMEM((nk * pitch, nk), F32)],
        compiler_params=_cparams(("arbitrary",)),
        name="peer_route",
    )(st, u, v)


def _peer_ffn_kernel(h2_ref, ue_ref, uo_ref, ve_ref, vo_ref, we_ref, wo_ref, h_ref, gf_ref, o_ref,
                     acc_scr, act0, act1):
    j = pl.program_id(1)
    last = pl.num_programs(1) - 1

    @pl.when(j == 0)
    def _():
        acc_scr[...] = jnp.zeros(acc_scr.shape, F32)
        act1[...] = jnp.zeros(act1.shape, F32)

    def consume(act_ref, w_ref, v_ref):
        z = []
        for c in range(w_ref.shape[0]):
            x = act_ref[:, c * LANES:(c + 1) * LANES]
            th = jnp.tanh(x * (GELU_C1 + GELU_C2 * (x * x)))
            z.append((x * th + x).astype(BF16) * w_ref[c])
        acc_scr[...] += _dot(jnp.concatenate(z, axis=1), v_ref[...])

    h2 = h2_ref[...]
    act0[...] = _dot_nt(h2, ue_ref[...])
    consume(act1, we_ref, ve_ref)

    @pl.when(j < last)
    def _():
        act1[...] = _dot_nt(h2, uo_ref[...])
        consume(act0, wo_ref, vo_ref)

    @pl.when(j == last)
    def _():
        h = h_ref[...] + acc_scr[...]
        ms = jnp.mean(h * h, axis=-1, keepdims=True)
        o_ref[...] = (h * lax.rsqrt(ms + EPS)) * gf_ref[...]


def _peer_ffn(h2, u, v, w, h, gf, tm, te):
    bt, d = h.shape
    ne = u.shape[0] // te
    assert ne % 2 == 0
    nb = te // LANES
    top = ne - 1
    produce_e = lambda i, j: (jnp.minimum(2 * j, top), 0)
    produce_o = lambda i, j: (jnp.minimum(2 * j + 1, top), 0)
    consume_e = lambda i, j: (jnp.maximum(2 * j - 1, 0), 0)
    consume_o = lambda i, j: (jnp.minimum(2 * j, top), 0)
    return pl.pallas_call(
        _peer_ffn_kernel,
        grid=(bt // tm, ne // 2 + 1),
        in_specs=[pl.BlockSpec((tm, d), lambda i, j: (i, 0)),
                  pl.BlockSpec((te, d), produce_e),
                  pl.BlockSpec((te, d), produce_o),
                  pl.BlockSpec((te, d), consume_e),
                  pl.BlockSpec((te, d), consume_o),
                  pl.BlockSpec((nb, tm, LANES), lambda i, j: (jnp.maximum(2 * j - 1, 0), i, 0)),
                  pl.BlockSpec((nb, tm, LANES), lambda i, j: (jnp.minimum(2 * j, top), i, 0)),
                  pl.BlockSpec((tm, d), lambda i, j: (i, 0)),
                  pl.BlockSpec((1, d), lambda i, j: (0, 0))],
        out_specs=pl.BlockSpec((tm, d), lambda i, j: (i, 0)),
        out_shape=jax.ShapeDtypeStruct((bt, d), F32),
        scratch_shapes=[pltpu.VMEM((tm, d), F32), pltpu.VMEM((tm, te), F32), pltpu.VMEM((tm, te), F32)],
        compiler_params=_cparams(("parallel", "arbitrary")),
        name="peer_ffn",
    )(h2, u, u, v, v, w, w, h, gf)


def _split_offsets():
    hd = HEAD_DIM
    sizes = [NSA_HEADS * hd, NSA_KV * hd, NSA_KV * hd, NSA_KV * hd, NSA_KV * hd, NSA_KV * hd,
             NSA_KV * hd, NSA_HEADS * 3, SWA_HEADS * hd, hd, hd]
    return [0] + [int(c) for c in np.cumsum(sizes)]


def kernel(x, ln1_g, w_in, cmp_pe_k, cmp_w1_k, cmp_w2_k, cmp_pe_v, cmp_w1_v, cmp_w2_v, swa_sinks, w_out, ln2_g, peer_wq, peer_keys, peer_u, peer_v, lnf_g):
    batch, seq, d = x.shape
    bt = batch * seq
    hd = HEAD_DIM
    nsa_tq, nsa_tk = NSA_TQ, NSA_TK
    assert ln1_g.shape[0] == 1, "single layer"
    assert seq % PROJ_TM == 0 and seq % NSA_TQ == 0 and seq % SWA_TQ == 0
    assert bt % MID_TM == 0 and bt % ROUTE_TT == 0 and bt % FFN_TM == 0
    slopes = jnp.asarray((2.0 ** (-8.0 * (np.arange(N_HEADS) + 1) / N_HEADS)).astype(np.float32))

    off = _split_offsets()
    w = w_in[0]
    col = lambda k: w[:, off[k]:off[k + 1]]
    scale = hd ** -0.5
    z64 = jnp.zeros((d, hd), F32)
    qn, qs = col(0) * (scale * LOG2E), col(8) * scale
    qn_exp = []
    for h in range(NSA_HEADS):
        qh = qn[:, h * hd:(h + 1) * hd]
        qn_exp += [qh, z64] if h // NSA_GROUP == 0 else [z64, qh]
    qs_exp = []
    for h in range(SWA_HEADS):
        qs_exp += [qs[:, h * hd:(h + 1) * hd], z64]
    gt = col(7)
    gpad = jnp.zeros((d, LANES - NSA_GROUP * 3), F32)
    gcols = []
    for g in range(NSA_KV):
        gcols += [gt[:, g * NSA_GROUP * 3:(g + 1) * NSA_GROUP * 3], gpad]
    w_all = jnp.concatenate(qn_exp + qs_exp + [col(3), col(5), col(9), col(10)]
                            + [col(4), col(6)] + [col(1), col(2)] + gcols, axis=1).astype(BF16)
    swa_q_col = NSA_HEADS * LANES
    nsa_k_col = swa_q_col + SWA_HEADS * LANES
    swa_kv_col = nsa_k_col + 2 * LANES
    n_bf = swa_kv_col + LANES
    n_vt = 2 * LANES

    x2 = x.reshape(bt, d)
    pbf, vt3, pf = _proj(x2, ln1_g[0][None, :], w_all, n_bf, n_vt, PROJ_TM, nsa_tk)

    ncp = seq // NSA_CMP_STRIDE
    pe_dup = jnp.stack([jnp.concatenate([pe, pe], axis=1) for pe in (cmp_pe_k[0], cmp_pe_v[0])])
    w1_dup = jnp.stack([jnp.concatenate([w1.reshape(NSA_CMP_LEN, hd, -1)] * 2, axis=1)
                        .reshape(2, NSA_CMP_STRIDE * LANES, -1)
                        for w1 in (cmp_w1_k[0], cmp_w1_v[0])]).astype(BF16)
    zc = jnp.zeros((NSA_CMP_HIDDEN, hd), F32)
    w2e = jnp.stack([jnp.stack([jnp.concatenate([w2, zc], axis=1), jnp.concatenate([zc, w2], axis=1)])
                     for w2 in (cmp_w2_k[0], cmp_w2_v[0])]).astype(BF16)
    kcmp, kcmpt = _compress(pf, pe_dup, w1_dup, w2e, batch, seq)

    n_sel = seq // NSA_SEL_BLOCK
    c0 = np.arange(ncp)[None, :] * NSA_CMP_STRIDE
    s0 = np.arange(n_sel)[:, None] * NSA_SEL_BLOCK
    ovt = ((c0 < s0 + NSA_SEL_BLOCK) & (c0 + NSA_CMP_LEN > s0) & (np.arange(ncp)[None, :] < ncp - 1))
    ovt = jnp.asarray(ovt.astype(np.float32), BF16)
    grp = (np.arange(n_sel)[None, :] * NSA_SEL_BLOCK // nsa_tk) == np.arange(seq // nsa_tk)[:, None]
    grp = jnp.asarray(grp.astype(np.float32), BF16)

    o_n = _nsa(slopes, pbf, vt3, pf, kcmp, kcmpt, ovt, grp, batch, seq, nsa_tq, nsa_tk, nsa_k_col)
    o_s = _swa(slopes, swa_sinks[0], pbf, batch, seq, SWA_TQ, swa_q_col, swa_kv_col)

    keys = peer_keys[0].reshape(2 * PEER_HEADS, PEER_NKEYS, -1).astype(BF16)
    h, h2, st = _mid(x2, o_n, o_s, w_out[0].astype(BF16), ln2_g[0][None, :],
                     peer_wq[0].astype(BF16), keys, MID_TM)
    wd, ub, vb = _peer_route(st, peer_u[0], peer_v[0], ROUTE_TT, ROUTE_GROUP)
    out = _peer_ffn(h2, ub, vb, wd, h, lnf_g[None, :], FFN_TM, FFN_TE)
    return out.reshape(batch, seq, d)
```
